```python
import math
import jax, jax.numpy as jnp
from jax import lax
import numpy as np

D_MODEL = 2048
BATCH = 2
SEQ = 8192
DEPTH = 4
DEC_BATCH = 1
DEC_SEQ = 16384
PAST_LEN = 128

N_MIXERS = 2
N_CONV_LAYERS = (DEPTH + 1) // 2
N_MLA_LAYERS = DEPTH // 2
CONV_EXPAND = 2
CONV_WIDTH = CONV_EXPAND * D_MODEL
CONV_KERNEL = 31
CONV_IN = 3 * CONV_WIDTH
N_HEADS = 16
QK_NOPE_DIM = 128
QK_ROPE_DIM = 64
V_HEAD_DIM = 128
Q_LORA_RANK = 512
KV_LORA_RANK = 512
MLA_WIDTH = N_HEADS * V_HEAD_DIM
MLA_IN = Q_LORA_RANK + KV_LORA_RANK + QK_ROPE_DIM + MLA_WIDTH
Q_BLOCK = 128
ROPE_THETA = 10000.0
EPS = 1e-6

kernel_name = "hybrid_conv_mla_adaln_encoder"


def rmsnorm(x, g):
    xf = x.astype(jnp.float32)
    y = xf * lax.rsqrt(jnp.mean(xf * xf, axis=-1, keepdims=True) + EPS)
    return (y * g.astype(jnp.float32)).astype(x.dtype)


def layernorm(x, g, b):
    xf = x.astype(jnp.float32)
    mu = jnp.mean(xf, axis=-1, keepdims=True)
    xc = xf - mu
    var = jnp.mean(xc * xc, axis=-1, keepdims=True)
    y = xc * lax.rsqrt(var + EPS) * g.astype(jnp.float32) + b.astype(jnp.float32)
    return y.astype(x.dtype)


def rope_tables(length):
    inv = 1.0 / (ROPE_THETA ** (jnp.arange(0, QK_ROPE_DIM, 2, dtype=jnp.float32) / QK_ROPE_DIM))
    ang = jnp.arange(length, dtype=jnp.float32)[:, None] * inv[None, :]
    return jnp.cos(ang), jnp.sin(ang)


def apply_rope(x, cos, sin):
    xf = x.astype(jnp.float32)
    x1, x2 = jnp.split(xf, 2, axis=-1)
    return jnp.concatenate([x1 * cos - x2 * sin, x2 * cos + x1 * sin], axis=-1).astype(x.dtype)


def depthwise_conv(x, w, b):
    pad = CONV_KERNEL // 2
    y = lax.conv_general_dilated(x, w[:, None, :].astype(x.dtype), window_strides=(1,),
                                 padding=[(pad, pad)], dimension_numbers=('NWC', 'WIO', 'NWC'),
                                 feature_group_count=x.shape[-1])
    return y + b.astype(x.dtype)


def conv_mixer(h, w_in, conv_w, conv_b, ln_g, ln_b, w_out):
    u = jnp.einsum('bld,de->ble', h, w_in)
    a, gl, z = jnp.split(u, 3, axis=-1)
    y = a * jax.nn.sigmoid(gl)
    y = depthwise_conv(y, conv_w, conv_b)
    y = jax.nn.silu(layernorm(y, ln_g, ln_b))
    y = y * jax.nn.silu(z)
    return jnp.einsum('blc,cd->bld', y, w_out)


def mla_attention(q_nope, q_rope, k_nope, k_rope, v):
    bsz, length = q_nope.shape[0], q_nope.shape[1]
    nb = length // Q_BLOCK
    scale = 1.0 / math.sqrt(QK_NOPE_DIM + QK_ROPE_DIM)

    def blocks(t):
        return t.reshape(bsz, nb, Q_BLOCK, *t.shape[2:]).swapaxes(0, 1)

    def one_block(args):
        qn, qr = args
        s = (jnp.einsum('bqhd,bkhd->bhqk', qn, k_nope)
             + jnp.einsum('bqhr,bkr->bhqk', qr, k_rope))
        p = jax.nn.softmax(s.astype(jnp.float32) * scale, axis=-1).astype(v.dtype)
        return jnp.einsum('bhqk,bkhd->bqhd', p, v)

    o = lax.map(one_block, (blocks(q_nope), blocks(q_rope)))
    return o.swapaxes(0, 1).reshape(bsz, length, MLA_WIDTH)


def mla_mixer(h, w_in, q_norm, kv_norm, w_uq, w_ukv, w_out, cos, sin):
    bsz, length = h.shape[0], h.shape[1]
    u = jnp.einsum('bld,de->ble', h, w_in)
    c_q, c_kv, k_r, z = jnp.split(
        u, [Q_LORA_RANK, Q_LORA_RANK + KV_LORA_RANK, Q_LORA_RANK + KV_LORA_RANK + QK_ROPE_DIM], axis=-1)
    c_q = rmsnorm(c_q, q_norm)
    c_kv = rmsnorm(c_kv, kv_norm)
    q = jnp.einsum('blr,re->ble', c_q, w_uq).reshape(bsz, length, N_HEADS, QK_NOPE_DIM + QK_ROPE_DIM)
    kv = jnp.einsum('blr,re->ble', c_kv, w_ukv).reshape(bsz, length, N_HEADS, QK_NOPE_DIM + V_HEAD_DIM)
    q_nope, q_rope = q[..., :QK_NOPE_DIM], q[..., QK_NOPE_DIM:]
    k_nope, v = kv[..., :QK_NOPE_DIM], kv[..., QK_NOPE_DIM:]
    q_rope = apply_rope(q_rope, cos[:, None, :], sin[:, None, :])
    k_rope = apply_rope(k_r, cos, sin)
    o = mla_attention(q_nope, q_rope, k_nope, k_rope, v)
    o = o * jax.nn.silu(z)
    return jnp.einsum('ble,ed->bld', o, w_out)


def trunk(x, c, norm_g, ada_w, ada_b,
          conv_w_in, conv_dw_w, conv_dw_b, conv_ln_g, conv_ln_b, conv_w_out,
          mla_w_in, mla_q_norm, mla_kv_norm, mla_w_uq, mla_w_ukv, mla_w_out, final_g):
    cos, sin = rope_tables(x.shape[1])
    c_act = jax.nn.silu(c)
    for i in range(DEPTH):
        mod = jnp.einsum('bd,de->be', c_act, ada_w[i]) + ada_b[i]
        shift, scale, gate = jnp.split(mod, 3, axis=-1)
        h = rmsnorm(x, norm_g[i]) * (1.0 + scale[:, None, :]) + shift[:, None, :]
        j = i // N_MIXERS
        if i % N_MIXERS == 0:
            y = conv_mixer(h, conv_w_in[j], conv_dw_w[j], conv_dw_b[j], conv_ln_g[j], conv_ln_b[j], conv_w_out[j])
        else:
            y = mla_mixer(h, mla_w_in[j], mla_q_norm[j], mla_kv_norm[j], mla_w_uq[j], mla_w_ukv[j], mla_w_out[j], cos, sin)
        x = x + gate[:, None, :] * y
    return rmsnorm(x, final_g)


def setup_inputs(seed: int = 0) -> dict:
    key = jax.random.key(seed)
    ks = jax.random.split(key, 24)
    f32 = jnp.float32

    def nrm(k, shape, std):
        return jax.random.normal(k, shape, f32) * std

    return {
        "x_prompt": nrm(ks[0], (BATCH, SEQ, D_MODEL), 1.0),
        "x_sample": nrm(ks[1], (DEC_BATCH, DEC_SEQ, D_MODEL), 1.0),
        "c_prompt": nrm(ks[2], (BATCH, D_MODEL), 1.0),
        "c_sample": nrm(ks[3], (DEC_BATCH, D_MODEL), 1.0),
        "norm_g": 1.0 + nrm(ks[4], (DEPTH, D_MODEL), 0.02),
        "ada_w": nrm(ks[5], (DEPTH, D_MODEL, 3 * D_MODEL), 0.5 * D_MODEL ** -0.5),
        "ada_b": nrm(ks[6], (DEPTH, 3 * D_MODEL), 0.01),
        "conv_w_in": nrm(ks[7], (N_CONV_LAYERS, D_MODEL, CONV_IN), D_MODEL ** -0.5),
        "conv_dw_w": nrm(ks[8], (N_CONV_LAYERS, CONV_KERNEL, CONV_WIDTH), CONV_KERNEL ** -0.5),
        "conv_dw_b": nrm(ks[9], (N_CONV_LAYERS, CONV_WIDTH), 0.01),
        "conv_ln_g": 1.0 + nrm(ks[10], (N_CONV_LAYERS, CONV_WIDTH), 0.02),
        "conv_ln_b": nrm(ks[11], (N_CONV_LAYERS, CONV_WIDTH), 0.01),
        "conv_w_out": nrm(ks[12], (N_CONV_LAYERS, CONV_WIDTH, D_MODEL), CONV_WIDTH ** -0.5),
        "mla_w_in": nrm(ks[13], (N_MLA_LAYERS, D_MODEL, MLA_IN), D_MODEL ** -0.5),
        "mla_q_norm": 1.0 + nrm(ks[14], (N_MLA_LAYERS, Q_LORA_RANK), 0.02),
        "mla_kv_norm": 1.0 + nrm(ks[15], (N_MLA_LAYERS, KV_LORA_RANK), 0.02),
        "mla_w_uq": nrm(ks[16], (N_MLA_LAYERS, Q_LORA_RANK, N_HEADS * (QK_NOPE_DIM + QK_ROPE_DIM)), Q_LORA_RANK ** -0.5),
        "mla_w_ukv": nrm(ks[17], (N_MLA_LAYERS, KV_LORA_RANK, N_HEADS * (QK_NOPE_DIM + V_HEAD_DIM)), KV_LORA_RANK ** -0.5),
        "mla_w_out": nrm(ks[18], (N_MLA_LAYERS, MLA_WIDTH, D_MODEL), MLA_WIDTH ** -0.5),
        "final_g": 1.0 + nrm(ks[19], (D_MODEL,), 0.02),
    }


def reference(x_prompt, x_sample, c_prompt, c_sample, norm_g, ada_w, ada_b,
              conv_w_in, conv_dw_w, conv_dw_b, conv_ln_g, conv_ln_b, conv_w_out,
              mla_w_in, mla_q_norm, mla_kv_norm, mla_w_uq, mla_w_ukv, mla_w_out, final_g):
    y_prompt = trunk(x_prompt, c_prompt, norm_g, ada_w, ada_b,
                     conv_w_in, conv_dw_w, conv_dw_b, conv_ln_g, conv_ln_b, conv_w_out,
                     mla_w_in, mla_q_norm, mla_kv_norm, mla_w_uq, mla_w_ukv, mla_w_out, final_g)
    y_sample = trunk(x_sample, c_sample, norm_g, ada_w, ada_b,
                     conv_w_in, conv_dw_w, conv_dw_b, conv_ln_g, conv_ln_b, conv_w_out,
                     mla_w_in, mla_q_norm, mla_kv_norm, mla_w_uq, mla_w_ukv, mla_w_out, final_g)
    return (y_prompt, y_sample)
```

```python
import functools
import math

import jax
import jax.numpy as jnp
from jax import lax
from jax.experimental import pallas as pl
from jax.experimental.pallas import tpu as pltpu

F32 = jnp.float32
BF16 = jnp.bfloat16

D_MODEL = 2048
DEPTH = 4
CONV_WIDTH = 4096
CONV_KERNEL = 31
CONV_HALO = 16
N_HEADS = 16
QK_NOPE_DIM = 128
QK_ROPE_DIM = 64
HALF_ROPE = QK_ROPE_DIM // 2
QK_DIM = QK_NOPE_DIM + QK_ROPE_DIM
V_HEAD_DIM = 128
Q_LORA_RANK = 512
KV_LORA_RANK = 512
MLA_WIDTH = N_HEADS * V_HEAD_DIM
ROPE_THETA = 10000.0
EPS = 1e-6
LANES = 128
MOD_ROWS = 8
VMEM_LIMIT_BYTES = 56 * 1024 * 1024
Q_PRESCALE = math.log2(math.e) / math.sqrt(QK_DIM)
NT_DIMS = (((1,), (1,)), ((), ()))


def _params(*sem):
    return pltpu.CompilerParams(dimension_semantics=sem, vmem_limit_bytes=VMEM_LIMIT_BYTES)


def _silu(v):
    return v * jax.nn.sigmoid(v)


def _ada_body(c_ref, w_ref, b_ref, o_ref):
    c = c_ref[...]
    ca = _silu(c).astype(BF16)
    o_ref[0] = jnp.dot(ca, w_ref[0].astype(BF16), preferred_element_type=F32) + b_ref[0]


def _ada_mod(c_all, ada_w, ada_b):
    n = 3 * D_MODEL
    tn = 1024
    return pl.pallas_call(
        _ada_body,
        grid=(DEPTH, n // tn),
        in_specs=[pl.BlockSpec((MOD_ROWS, D_MODEL), lambda i, j: (0, 0)),
                  pl.BlockSpec((1, D_MODEL, tn), lambda i, j: (i, 0, j)),
                  pl.BlockSpec((1, 1, tn), lambda i, j: (i, 0, j))],
        out_specs=pl.BlockSpec((1, MOD_ROWS, tn), lambda i, j: (i, 0, j)),
        out_shape=jax.ShapeDtypeStruct((DEPTH, MOD_ROWS, n), F32),
        compiler_params=_params("parallel", "parallel"),
        name="ada_mod",
    )(c_all, ada_w, ada_b.reshape(DEPTH, 1, n))


def _modnorm_to(x_ref, g_ref, sc_ref, sh_ref, h_ref, tm, chunk):
    g = g_ref[...]
    sc = 1.0 + sc_ref[...]
    sh = sh_ref[...]

    def body(c, carry):
        r = pl.multiple_of(c * chunk, chunk)
        x = x_ref[pl.ds(r, chunk), :]
        ms = jnp.mean(x * x, axis=-1, keepdims=True)
        y = x * lax.rsqrt(ms + EPS) * g
        h_ref[pl.ds(r, chunk), :] = (y * sc + sh).astype(BF16)
        return carry

    lax.fori_loop(0, tm // chunk, body, 0)


def _conv_in_body(x_ref, g_ref, sc_ref, sh_ref, wa_ref, wg_ref, wz_ref, y_ref, z_ref, h_ref, *, tm):
    @pl.when(pl.program_id(2) == 0)
    def _():
        _modnorm_to(x_ref, g_ref, sc_ref, sh_ref, h_ref, tm, min(tm, 64))

    h = h_ref[...]
    a = jnp.dot(h, wa_ref[...], preferred_element_type=F32)
    gl = jnp.dot(h, wg_ref[...], preferred_element_type=F32)
    y_ref[...] = a * jax.nn.sigmoid(gl)
    z = jnp.dot(h, wz_ref[...], preferred_element_type=F32)
    z_ref[...] = _silu(z)


def _conv_in(x, g, scale, shift, w_in):
    b, l, _ = x.shape
    c = CONV_WIDTH
    tm = min(512, l)
    tn = 512
    nj = c // tn
    return pl.pallas_call(
        functools.partial(_conv_in_body, tm=tm),
        grid=(b, l // tm, nj),
        in_specs=[pl.BlockSpec((None, tm, D_MODEL), lambda bi, i, j: (bi, i, 0)),
                  pl.BlockSpec((1, D_MODEL), lambda bi, i, j: (0, 0)),
                  pl.BlockSpec((None, 1, D_MODEL), lambda bi, i, j: (bi, 0, 0)),
                  pl.BlockSpec((None, 1, D_MODEL), lambda bi, i, j: (bi, 0, 0)),
                  pl.BlockSpec((D_MODEL, tn), lambda bi, i, j: (0, j)),
                  pl.BlockSpec((D_MODEL, tn), lambda bi, i, j: (0, j + nj)),
                  pl.BlockSpec((D_MODEL, tn), lambda bi, i, j: (0, j + 2 * nj))],
        out_specs=[pl.BlockSpec((None, tm, tn), lambda bi, i, j: (bi, i, j)),
                   pl.BlockSpec((None, tm, tn), lambda bi, i, j: (bi, i, j))],
        out_shape=[jax.ShapeDtypeStruct((b, l, c), F32), jax.ShapeDtypeStruct((b, l, c), F32)],
        scratch_shapes=[pltpu.VMEM((tm, D_MODEL), BF16)],
        compiler_params=_params("parallel", "parallel", "arbitrary"),
        name="conv_in",
    )(x, g, scale, shift, w_in, w_in, w_in)


def _dwconv_body(yp_ref, ym_ref, yn_ref, zs_ref, w_ref, b_ref, lg_ref, lb_ref, o_ref,
                 xbuf, shbuf, cbuf, *, tl, strip, ln_rows):
    i = pl.program_id(1)
    n = pl.num_programs(1)
    halo = CONV_HALO
    xbuf[0:halo, :] = jnp.where(i > 0, yp_ref[...], 0.0)
    xbuf[halo:halo + tl, :] = ym_ref[...]
    xbuf[halo + tl:2 * halo + tl, :] = jnp.where(i < n - 1, yn_ref[...], 0.0)

    first = halo - CONV_KERNEL // 2
    sh_rows = tl + 24

    def cb_body(cb, carry):
        lanes = pl.ds(pl.multiple_of(cb * LANES, LANES), LANES)
        for r in range(1, 8):
            shbuf[r, :, :] = xbuf[pl.ds(r, sh_rows), lanes]
        bias = b_ref[:, lanes]
        for s in range(tl // strip):
            acc = jnp.broadcast_to(bias, (strip, LANES))
            for k in range(CONV_KERNEL):
                a8, r = divmod(first + k, 8)
                start = s * strip + 8 * a8
                if r == 0:
                    win = xbuf[pl.ds(start, strip), lanes]
                else:
                    win = shbuf[r, pl.ds(start, strip), :]
                acc = acc + win * w_ref[k:k + 1, lanes]
            cbuf[pl.ds(s * strip, strip), lanes] = acc
        return carry

    lax.fori_loop(0, CONV_WIDTH // LANES, cb_body, 0)

    lg = lg_ref[...]
    lb = lb_ref[...]

    def ln_body(t, carry):
        r0 = pl.multiple_of(t * ln_rows, ln_rows)
        v = cbuf[pl.ds(r0, ln_rows), :]
        mu = jnp.mean(v, axis=-1, keepdims=True)
        vc = v - mu
        var = jnp.mean(vc * vc, axis=-1, keepdims=True)
        yn = vc * lax.rsqrt(var + EPS) * lg + lb
        o_ref[pl.ds(r0, ln_rows), :] = (_silu(yn) * zs_ref[pl.ds(r0, ln_rows), :]).astype(BF16)
        return carry

    lax.fori_loop(0, tl // ln_rows, ln_body, 0)


def _dwconv(y, zs, dw_w, dw_b, ln_g, ln_b):
    b, l, c = y.shape
    tl = min(256, l)
    hb = tl // CONV_HALO
    nhb = l // CONV_HALO
    body = functools.partial(_dwconv_body, tl=tl, strip=min(64, tl), ln_rows=16)
    row = lambda bi, i: (0, 0)
    return pl.pallas_call(
        body,
        grid=(b, l // tl),
        in_specs=[pl.BlockSpec((None, CONV_HALO, c), lambda bi, i: (bi, jnp.maximum(i * hb - 1, 0), 0)),
                  pl.BlockSpec((None, tl, c), lambda bi, i: (bi, i, 0)),
                  pl.BlockSpec((None, CONV_HALO, c), lambda bi, i: (bi, jnp.minimum((i + 1) * hb, nhb - 1), 0)),
                  pl.BlockSpec((None, tl, c), lambda bi, i: (bi, i, 0)),
                  pl.BlockSpec((CONV_KERNEL, c), row),
                  pl.BlockSpec((1, c), row),
                  pl.BlockSpec((1, c), row),
                  pl.BlockSpec((1, c), row)],
        out_specs=pl.BlockSpec((None, tl, c), lambda bi, i: (bi, i, 0)),
        out_shape=jax.ShapeDtypeStruct((b, l, c), BF16),
        scratch_shapes=[pltpu.VMEM((tl + 2 * CONV_HALO, c), F32),
                        pltpu.VMEM((8, tl + 24, LANES), F32),
                        pltpu.VMEM((tl, c), F32)],
        compiler_params=_params("parallel", "parallel"),
        name="dwconv_ln",
    )(y, y, y, zs, dw_w, dw_b.reshape(1, c), ln_g.reshape(1, c), ln_b.reshape(1, c))


def _out_proj_body(a_ref, w_ref, x_ref, gt_ref, o_ref):
    y = jnp.dot(a_ref[...], w_ref[...], preferred_element_type=F32)
    o_ref[...] = x_ref[...] + gt_ref[...] * y


def _out_proj(a, w, x, gate):
    b, l, k = a.shape
    tm = min(1024, l)
    tn = 512
    return pl.pallas_call(
        _out_proj_body,
        grid=(b, l // tm, D_MODEL // tn),
        in_specs=[pl.BlockSpec((None, tm, k), lambda bi, i, j: (bi, i, 0)),
                  pl.BlockSpec((k, tn), lambda bi, i, j: (0, j)),
                  pl.BlockSpec((None, tm, tn), lambda bi, i, j: (bi, i, j)),
                  pl.BlockSpec((None, 1, tn), lambda bi, i, j: (bi, 0, j))],
        out_specs=pl.BlockSpec((None, tm, tn), lambda bi, i, j: (bi, i, j)),
        out_shape=jax.ShapeDtypeStruct((b, l, D_MODEL), F32),
        compiler_params=_params("parallel", "parallel", "parallel"),
        name="out_proj",
    )(a, w, x, gate)


def _mla_in_body(x_ref, g_ref, sc_ref, sh_ref, wq_ref, wkv_ref, wka_ref, wkb_ref, wz_ref,
                 qn_ref, kvn_ref, cc_ref, ss_ref,
                 cq_out, ckv_out, kr_out, zs_out, h_ref, *, tm, zchunk):
    _modnorm_to(x_ref, g_ref, sc_ref, sh_ref, h_ref, tm, min(tm, 64))
    h = h_ref[...]

    def rms(v, gain):
        ms = jnp.mean(v * v, axis=-1, keepdims=True)
        return (v * lax.rsqrt(ms + EPS) * gain).astype(BF16)

    cq_out[...] = rms(jnp.dot(h, wq_ref[...], preferred_element_type=F32), qn_ref[...])
    ckv_out[...] = rms(jnp.dot(h, wkv_ref[...], preferred_element_type=F32), kvn_ref[...])
    ka = jnp.dot(h, wka_ref[...], preferred_element_type=F32)
    kb = jnp.dot(h, wkb_ref[...], preferred_element_type=F32)
    kr = ka * cc_ref[...] + kb * ss_ref[...]
    kr_out[...] = kr[:, :QK_ROPE_DIM].astype(BF16)
    for c in range(MLA_WIDTH // zchunk):
        z = jnp.dot(h, wz_ref[:, c * zchunk:(c + 1) * zchunk], preferred_element_type=F32)
        zs_out[:, c * zchunk:(c + 1) * zchunk] = _silu(z)


def _mla_in(x, g, scale, shift, wq, wkv, wka, wkb, wz, q_norm, kv_norm, cc, ss):
    b, l, _ = x.shape
    tm = min(256, l)
    const = lambda bi, i: (0, 0)
    rows = lambda bi, i: (bi, i, 0)
    per_b = lambda bi, i: (bi, 0, 0)
    return pl.pallas_call(
        functools.partial(_mla_in_body, tm=tm, zchunk=512),
        grid=(b, l // tm),
        in_specs=[pl.BlockSpec((None, tm, D_MODEL), rows),
                  pl.BlockSpec((1, D_MODEL), const),
                  pl.BlockSpec((None, 1, D_MODEL), per_b),
                  pl.BlockSpec((None, 1, D_MODEL), per_b),
                  pl.BlockSpec((D_MODEL, Q_LORA_RANK), const),
                  pl.BlockSpec((D_MODEL, KV_LORA_RANK), const),
                  pl.BlockSpec((D_MODEL, LANES), const),
                  pl.BlockSpec((D_MODEL, LANES), const),
                  pl.BlockSpec((D_MODEL, MLA_WIDTH), const),
                  pl.BlockSpec((1, Q_LORA_RANK), const),
                  pl.BlockSpec((1, KV_LORA_RANK), const),
                  pl.BlockSpec((tm, LANES), lambda bi, i: (i, 0)),
                  pl.BlockSpec((tm, LANES), lambda bi, i: (i, 0))],
        out_specs=[pl.BlockSpec((None, tm, Q_LORA_RANK), rows),
                   pl.BlockSpec((None, tm, KV_LORA_RANK), rows),
                   pl.BlockSpec((None, tm, QK_ROPE_DIM), rows),
                   pl.BlockSpec((None, tm, MLA_WIDTH), rows)],
        out_shape=[jax.ShapeDtypeStruct((b, l, Q_LORA_RANK), BF16),
                   jax.ShapeDtypeStruct((b, l, KV_LORA_RANK), BF16),
                   jax.ShapeDtypeStruct((b, l, QK_ROPE_DIM), BF16),
                   jax.ShapeDtypeStruct((b, l, MLA_WIDTH), F32)],
        scratch_shapes=[pltpu.VMEM((tm, D_MODEL), BF16)],
        compiler_params=_params("parallel", "parallel"),
        name="mla_in",
    )(x, g, scale, shift, wq, wkv, wka, wkb, wz, q_norm, kv_norm, cc, ss)


def _mla_up_body(cq_ref, ckv_ref, kr_ref, cost_ref, sint_ref, wuqt_ref, wuk_ref, wuvt_ref,
                 qt_out, k_out, vt_out):
    cq = cq_ref[...]
    ckv = ckv_ref[...]
    cost = cost_ref[...]
    sint = sint_ref[...]
    qt = lax.dot_general(wuqt_ref[...], cq, NT_DIMS, preferred_element_type=F32)
    for h in range(N_HEADS):
        r0 = h * QK_DIM
        r1 = r0 + QK_NOPE_DIM
        r2 = r1 + HALF_ROPE
        r3 = r2 + HALF_ROPE
        x1 = qt[r1:r2]
        x2 = qt[r2:r3]
        qt_out[r0:r1, :] = (qt[r0:r1] * Q_PRESCALE).astype(BF16)
        qt_out[r1:r2, :] = ((x1 * cost - x2 * sint) * Q_PRESCALE).astype(BF16)
        qt_out[r2:r3, :] = ((x2 * cost + x1 * sint) * Q_PRESCALE).astype(BF16)
    kn = jnp.dot(ckv, wuk_ref[...], preferred_element_type=F32)
    kr = kr_ref[...]
    for h in range(N_HEADS):
        k_out[h, :, 0:QK_NOPE_DIM] = kn[:, h * QK_NOPE_DIM:(h + 1) * QK_NOPE_DIM].astype(BF16)
        k_out[h, :, QK_NOPE_DIM:QK_DIM] = kr
    vt = lax.dot_general(wuvt_ref[...], ckv, NT_DIMS, preferred_element_type=F32)
    vt_out[...] = vt.astype(BF16)


def _mla_up(cq, ckv, kr, cost, sint, wuqt, wuk, wuvt):
    b, l, _ = cq.shape
    tm = min(256, l)
    const = lambda bi, i: (0, 0)
    rows = lambda bi, i: (bi, i, 0)
    cols = lambda bi, i: (bi, 0, i)
    return pl.pallas_call(
        _mla_up_body,
        grid=(b, l // tm),
        in_specs=[pl.BlockSpec((None, tm, Q_LORA_RANK), rows),
                  pl.BlockSpec((None, tm, KV_LORA_RANK), rows),
                  pl.BlockSpec((None, tm, QK_ROPE_DIM), rows),
                  pl.BlockSpec((HALF_ROPE, tm), lambda bi, i: (0, i)),
                  pl.BlockSpec((HALF_ROPE, tm), lambda bi, i: (0, i)),
                  pl.BlockSpec((N_HEADS * QK_DIM, Q_LORA_RANK), const),
                  pl.BlockSpec((KV_LORA_RANK, N_HEADS * QK_NOPE_DIM), const),
                  pl.BlockSpec((MLA_WIDTH, KV_LORA_RANK), const)],
        out_specs=[pl.BlockSpec((None, N_HEADS * QK_DIM, tm), cols),
                   pl.BlockSpec((None, N_HEADS, tm, QK_DIM), lambda bi, i: (bi, 0, i, 0)),
                   pl.BlockSpec((None, MLA_WIDTH, tm), cols)],
        out_shape=[jax.ShapeDtypeStruct((b, N_HEADS * QK_DIM, l), BF16),
                   jax.ShapeDtypeStruct((b, N_HEADS, l, QK_DIM), BF16),
                   jax.ShapeDtypeStruct((b, MLA_WIDTH, l), BF16)],
        compiler_params=_params("parallel", "parallel"),
        name="mla_up",
    )(cq, ckv, kr, cost, sint, wuqt, wuk, wuvt)


def _attn_body(qt_ref, k_ref, vt_ref, zs_ref, o_ref, acc_ref, m_ref, l_ref, *, seq, tk):
    qt = qt_ref[...]
    m_ref[...] = jnp.full(m_ref.shape, -jnp.inf, F32)
    l_ref[...] = jnp.zeros(l_ref.shape, F32)
    acc_ref[...] = jnp.zeros(acc_ref.shape, F32)

    def body(c, carry):
        r = pl.multiple_of(c * tk, tk)
        s = jnp.dot(k_ref[pl.ds(r, tk), :], qt, preferred_element_type=F32)
        m_prev = m_ref[...]
        m_new = jnp.maximum(m_prev, jnp.max(s, axis=0, keepdims=True))
        alpha = jnp.exp2(m_prev - m_new)
        p = jnp.exp2(s - m_new)
        l_ref[...] = alpha * l_ref[...] + jnp.sum(p, axis=0, keepdims=True)
        pv = jnp.dot(vt_ref[:, pl.ds(r, tk)], p.astype(BF16), preferred_element_type=F32)
        acc_ref[...] = alpha * acc_ref[...] + pv
        m_ref[...] = m_new
        return carry

    lax.fori_loop(0, seq // tk, body, 0)
    o = acc_ref[...] / l_ref[...]
    o_ref[...] = (o.T * zs_ref[...]).astype(BF16)


def _attention(qt, k, vt, zs):
    b, _, l = qt.shape
    tq = min(512, l)
    tk = min(512, l)
    return pl.pallas_call(
        functools.partial(_attn_body, seq=l, tk=tk),
        grid=(b, N_HEADS, l // tq),
        in_specs=[pl.BlockSpec((None, QK_DIM, tq), lambda bi, h, qi: (bi, h, qi)),
                  pl.BlockSpec((None, None, l, QK_DIM), lambda bi, h, qi: (bi, h, 0, 0)),
                  pl.BlockSpec((None, V_HEAD_DIM, l), lambda bi, h, qi: (bi, h, 0)),
                  pl.BlockSpec((None, tq, V_HEAD_DIM), lambda bi, h, qi: (bi, qi, h))],
        out_specs=pl.BlockSpec((None, tq, V_HEAD_DIM), lambda bi, h, qi: (bi, qi, h)),
        out_shape=jax.ShapeDtypeStruct((b, l, MLA_WIDTH), BF16),
        scratch_shapes=[pltpu.VMEM((V_HEAD_DIM, tq), F32),
                        pltpu.VMEM((1, tq), F32),
                        pltpu.VMEM((1, tq), F32)],
        compiler_params=_params("parallel", "parallel", "parallel"),
        name="mla_attn",
    )(qt, k, vt, zs)


def _final_norm_body(x_ref, g_ref, o_ref):
    x = x_ref[...]
    ms = jnp.mean(x * x, axis=-1, keepdims=True)
    o_ref[...] = x * lax.rsqrt(ms + EPS) * g_ref[...]


def _final_norm(x, g):
    b, l, _ = x.shape
    tm = min(256, l)
    return pl.pallas_call(
        _final_norm_body,
        grid=(b, l // tm),
        in_specs=[pl.BlockSpec((None, tm, D_MODEL), lambda bi, i: (bi, i, 0)),
                  pl.BlockSpec((1, D_MODEL), lambda bi, i: (0, 0))],
        out_specs=pl.BlockSpec((None, tm, D_MODEL), lambda bi, i: (bi, i, 0)),
        out_shape=jax.ShapeDtypeStruct(x.shape, F32),
        compiler_params=_params("parallel", "parallel"),
        name="final_norm",
    )(x, g)


def _rope_tables(length):
    inv = 1.0 / (ROPE_THETA ** (jnp.arange(0, QK_ROPE_DIM, 2, dtype=F32) / QK_ROPE_DIM))
    ang = jnp.arange(length, dtype=F32)[:, None] * inv[None, :]
    return jnp.cos(ang), jnp.sin(ang)


def _prep_weights(conv_w_in, conv_w_out, mla_w_in, mla_w_uq, mla_w_ukv, mla_w_out):
    q0, q1, q2 = Q_LORA_RANK, Q_LORA_RANK + KV_LORA_RANK, Q_LORA_RANK + KV_LORA_RANK + QK_ROPE_DIM
    pad = ((0, 0), (0, 0), (0, LANES - QK_ROPE_DIM))
    wkr = mla_w_in[:, :, q1:q2]
    wkr_swapped = jnp.concatenate([wkr[..., HALF_ROPE:], wkr[..., :HALF_ROPE]], axis=-1)
    n_mla = mla_w_ukv.shape[0]
    wukv = mla_w_ukv.reshape(n_mla, KV_LORA_RANK, N_HEADS, QK_NOPE_DIM + V_HEAD_DIM)
    return dict(
        conv_w_in=conv_w_in.astype(BF16),
        conv_w_out=conv_w_out.astype(BF16),
        wq=mla_w_in[:, :, :q0].astype(BF16),
        wkv=mla_w_in[:, :, q0:q1].astype(BF16),
        wka=jnp.pad(wkr, pad).astype(BF16),
        wkb=jnp.pad(wkr_swapped, pad).astype(BF16),
        wz=mla_w_in[:, :, q2:].astype(BF16),
        wuqt=jnp.swapaxes(mla_w_uq, 1, 2).astype(BF16),
        wuk=wukv[..., :QK_NOPE_DIM].reshape(n_mla, KV_LORA_RANK, N_HEADS * QK_NOPE_DIM).astype(BF16),
        wuvt=jnp.swapaxes(wukv[..., QK_NOPE_DIM:].reshape(n_mla, KV_LORA_RANK, MLA_WIDTH), 1, 2).astype(BF16),
        mla_w_out=mla_w_out.astype(BF16),
    )


def _trunk(x, mod, norm_g, wts, conv_dw_w, conv_dw_b, conv_ln_g, conv_ln_b,
           mla_q_norm, mla_kv_norm, final_g):
    length = x.shape[1]
    cos, sin = _rope_tables(length)
    lane_pad = ((0, 0), (0, LANES - QK_ROPE_DIM))
    cc = jnp.pad(jnp.concatenate([cos, cos], axis=-1), lane_pad)
    ss = jnp.pad(jnp.concatenate([-sin, sin], axis=-1), lane_pad)
    cost, sint = cos.T, sin.T
    for i in range(DEPTH):
        shift = mod[i, :, 0][:, None, :]
        scale = mod[i, :, 1][:, None, :]
        gate = mod[i, :, 2][:, None, :]
        g = norm_g[i][None, :]
        j = i // 2
        if i % 2 == 0:
            y, zs = _conv_in(x, g, scale, shift, wts["conv_w_in"][j])
            yc = _dwconv(y, zs, conv_dw_w[j], conv_dw_b[j], conv_ln_g[j], conv_ln_b[j])
            x = _out_proj(yc, wts["conv_w_out"][j], x, gate)
        else:
            cq, ckv, kr, zs = _mla_in(x, g, scale, shift, wts["wq"][j], wts["wkv"][j], wts["wka"][j],
                                      wts["wkb"][j], wts["wz"][j], mla_q_norm[j][None, :],
                                      mla_kv_norm[j][None, :], cc, ss)
            qt, k, vt = _mla_up(cq, ckv, kr, cost, sint, wts["wuqt"][j], wts["wuk"][j], wts["wuvt"][j])
            og = _attention(qt, k, vt, zs)
            x = _out_proj(og, wts["mla_w_out"][j], x, gate)
    return _final_norm(x, final_g[None, :])


def kernel(x_prompt, x_sample, c_prompt, c_sample, norm_g, ada_w, ada_b, conv_w_in, conv_dw_w, conv_dw_b,
           conv_ln_g, conv_ln_b, conv_w_out, mla_w_in, mla_q_norm, mla_kv_norm, mla_w_uq, mla_w_ukv,
           mla_w_out, final_g):
    nb_p, nb_s = c_prompt.shape[0], c_sample.shape[0]
    c_all = jnp.concatenate([c_prompt, c_sample, jnp.zeros((MOD_ROWS - nb_p - nb_s, D_MODEL), F32)], axis=0)
    mod = _ada_mod(c_all, ada_w, ada_b).reshape(DEPTH, MOD_ROWS, 3, D_MODEL)
    wts = _prep_weights(conv_w_in, conv_w_out, mla_w_in, mla_w_uq, mla_w_ukv, mla_w_out)
    rest = (norm_g, wts, conv_dw_w, conv_dw_b, conv_ln_g, conv_ln_b, mla_q_norm, mla_kv_norm, final_g)
    y_prompt = _trunk(x_prompt, mod[:, :nb_p], *rest)
    y_sample = _trunk(x_sample, mod[:, nb_p:nb_p + nb_s], *rest)
    return (y_prompt, y_sample)
```

```python
import functools
import math

import jax
import jax.numpy as jnp
from jax import lax
from jax.experimental import pallas as pl
from jax.experimental.pallas import tpu as pltpu

F32 = jnp.float32
BF16 = jnp.bfloat16

D_MODEL = 2048
DEPTH = 4
CONV_WIDTH = 4096
CONV_KERNEL = 31
CONV_HALO = 16
N_HEADS = 16
QK_NOPE_DIM = 128
QK_ROPE_DIM = 64
HALF_ROPE = QK_ROPE_DIM // 2
QK_DIM = QK_NOPE_DIM + QK_ROPE_DIM
V_HEAD_DIM = 128
Q_LORA_RANK = 512
KV_LORA_RANK = 512
MLA_WIDTH = N_HEADS * V_HEAD_DIM
ROPE_THETA = 10000.0
EPS = 1e-6
LANES = 128
MOD_ROWS = 8
VMEM_LIMIT_BYTES = 56 * 1024 * 1024
Q_PRESCALE = math.log2(math.e) / math.sqrt(QK_DIM)
NT_DIMS = (((1,), (1,)), ((), ()))


def _params(*sem):
    return pltpu.CompilerParams(dimension_semantics=sem, vmem_limit_bytes=VMEM_LIMIT_BYTES)


def _silu(v):
    return v * jax.nn.sigmoid(v)


def _ada_body(c_ref, w_ref, b_ref, o_ref):
    c = c_ref[...]
    ca = _silu(c).astype(BF16)
    o_ref[0] = jnp.dot(ca, w_ref[0].astype(BF16), preferred_element_type=F32) + b_ref[0]


def _ada_mod(c_all, ada_w, ada_b):
    n = 3 * D_MODEL
    tn = 1024
    return pl.pallas_call(
        _ada_body,
        grid=(DEPTH, n // tn),
        in_specs=[pl.BlockSpec((MOD_ROWS, D_MODEL), lambda i, j: (0, 0)),
                  pl.BlockSpec((1, D_MODEL, tn), lambda i, j: (i, 0, j)),
                  pl.BlockSpec((1, 1, tn), lambda i, j: (i, 0, j))],
        out_specs=pl.BlockSpec((1, MOD_ROWS, tn), lambda i, j: (i, 0, j)),
        out_shape=jax.ShapeDtypeStruct((DEPTH, MOD_ROWS, n), F32),
        compiler_params=_params("parallel", "parallel"),
        name="ada_mod",
    )(c_all, ada_w, ada_b.reshape(DEPTH, 1, n))


def _modnorm_to(x_ref, g_ref, sc_ref, sh_ref, h_ref, tm, chunk):
    g = g_ref[...]
    sc = 1.0 + sc_ref[...]
    sh = sh_ref[...]

    def body(c, carry):
        r = pl.multiple_of(c * chunk, chunk)
        x = x_ref[pl.ds(r, chunk), :]
        ms = jnp.mean(x * x, axis=-1, keepdims=True)
        y = x * lax.rsqrt(ms + EPS) * g
        h_ref[pl.ds(r, chunk), :] = (y * sc + sh).astype(BF16)
        return carry

    lax.fori_loop(0, tm // chunk, body, 0)


def _conv_in_body(x_ref, g_ref, sc_ref, sh_ref, wa_ref, wg_ref, wz_ref, y_ref, z_ref, h_ref, *, tm):
    @pl.when(pl.program_id(2) == 0)
    def _():
        _modnorm_to(x_ref, g_ref, sc_ref, sh_ref, h_ref, tm, min(tm, 64))

    h = h_ref[...]
    a = jnp.dot(h, wa_ref[...], preferred_element_type=F32)
    gl = jnp.dot(h, wg_ref[...], preferred_element_type=F32)
    y_ref[...] = a * jax.nn.sigmoid(gl)
    z = jnp.dot(h, wz_ref[...], preferred_element_type=F32)
    z_ref[...] = _silu(z)


def _conv_in(x, g, scale, shift, w_in):
    b, l, _ = x.shape
    c = CONV_WIDTH
    tm = min(512, l)
    tn = 512
    nj = c // tn
    return pl.pallas_call(
        functools.partial(_conv_in_body, tm=tm),
        grid=(b, l // tm, nj),
        in_specs=[pl.BlockSpec((None, tm, D_MODEL), lambda bi, i, j: (bi, i, 0)),
                  pl.BlockSpec((1, D_MODEL), lambda bi, i, j: (0, 0)),
                  pl.BlockSpec((None, 1, D_MODEL), lambda bi, i, j: (bi, 0, 0)),
                  pl.BlockSpec((None, 1, D_MODEL), lambda bi, i, j: (bi, 0, 0)),
                  pl.BlockSpec((D_MODEL, tn), lambda bi, i, j: (0, j)),
                  pl.BlockSpec((D_MODEL, tn), lambda bi, i, j: (0, j + nj)),
                  pl.BlockSpec((D_MODEL, tn), lambda bi, i, j: (0, j + 2 * nj))],
        out_specs=[pl.BlockSpec((None, tm, tn), lambda bi, i, j: (bi, i, j)),
                   pl.BlockSpec((None, tm, tn), lambda bi, i, j: (bi, i, j))],
        out_shape=[jax.ShapeDtypeStruct((b, l, c), F32), jax.ShapeDtypeStruct((b, l, c), F32)],
        scratch_shapes=[pltpu.VMEM((tm, D_MODEL), BF16)],
        compiler_params=_params("parallel", "parallel", "arbitrary"),
        name="conv_in",
    )(x, g, scale, shift, w_in, w_in, w_in)


def _dwconv_body(yp_ref, ym_ref, yn_ref, zs_ref, w_ref, b_ref, lg_ref, lb_ref, o_ref,
                 xbuf, shbuf, cbuf, *, tl, strip, ln_rows):
    i = pl.program_id(1)
    n = pl.num_programs(1)
    halo = CONV_HALO
    xbuf[0:halo, :] = jnp.where(i > 0, yp_ref[...], 0.0)
    xbuf[halo:halo + tl, :] = ym_ref[...]
    xbuf[halo + tl:2 * halo + tl, :] = jnp.where(i < n - 1, yn_ref[...], 0.0)

    first = halo - CONV_KERNEL // 2
    sh_rows = tl + 24

    def cb_body(cb, carry):
        lanes = pl.ds(pl.multiple_of(cb * LANES, LANES), LANES)
        for r in range(1, 8):
            shbuf[r, :, :] = xbuf[pl.ds(r, sh_rows), lanes]
        bias = b_ref[:, lanes]
        for s in range(tl // strip):
            acc = jnp.broadcast_to(bias, (strip, LANES))
            for k in range(CONV_KERNEL):
                a8, r = divmod(first + k, 8)
                start = s * strip + 8 * a8
                if r == 0:
                    win = xbuf[pl.ds(start, strip), lanes]
                else:
                    win = shbuf[r, pl.ds(start, strip), :]
                acc = acc + win * w_ref[k:k + 1, lanes]
            cbuf[pl.ds(s * strip, strip), lanes] = acc
        return carry

    lax.fori_loop(0, CONV_WIDTH // LANES, cb_body, 0)

    lg = lg_ref[...]
    lb = lb_ref[...]

    def ln_body(t, carry):
        r0 = pl.multiple_of(t * ln_rows, ln_rows)
        v = cbuf[pl.ds(r0, ln_rows), :]
        mu = jnp.mean(v, axis=-1, keepdims=True)
        vc = v - mu
        var = jnp.mean(vc * vc, axis=-1, keepdims=True)
        yn = vc * lax.rsqrt(var + EPS) * lg + lb
        o_ref[pl.ds(r0, ln_rows), :] = (_silu(yn) * zs_ref[pl.ds(r0, ln_rows), :]).astype(BF16)
        return carry

    lax.fori_loop(0, tl // ln_rows, ln_body, 0)


def _dwconv(y, zs, dw_w, dw_b, ln_g, ln_b):
    b, l, c = y.shape
    tl = min(256, l)
    hb = tl // CONV_HALO
    nhb = l // CONV_HALO
    body = functools.partial(_dwconv_body, tl=tl, strip=min(64, tl), ln_rows=16)
    row = lambda bi, i: (0, 0)
    return pl.pallas_call(
        body,
        grid=(b, l // tl),
        in_specs=[pl.BlockSpec((None, CONV_HALO, c), lambda bi, i: (bi, jnp.maximum(i * hb - 1, 0), 0)),
                  pl.BlockSpec((None, tl, c), lambda bi, i: (bi, i, 0)),
                  pl.BlockSpec((None, CONV_HALO, c), lambda bi, i: (bi, jnp.minimum((i + 1) * hb, nhb - 1), 0)),
                  pl.BlockSpec((None, tl, c), lambda bi, i: (bi, i, 0)),
                  pl.BlockSpec((CONV_KERNEL, c), row),
                  pl.BlockSpec((1, c), row),
                  pl.BlockSpec((1, c), row),
                  pl.BlockSpec((1, c), row)],
        out_specs=pl.BlockSpec((None, tl, c), lambda bi, i: (bi, i, 0)),
        out_shape=jax.ShapeDtypeStruct((b, l, c), BF16),
        scratch_shapes=[pltpu.VMEM((tl + 2 * CONV_HALO, c), F32),
                        pltpu.VMEM((8, tl + 24, LANES), F32),
                        pltpu.VMEM((tl, c), F32)],
        compiler_params=_params("parallel", "parallel"),
        name="dwconv_ln",
    )(y, y, y, zs, dw_w, dw_b.reshape(1, c), ln_g.reshape(1, c), ln_b.reshape(1, c))


def _out_proj_body(a_ref, w_ref, x_ref, gt_ref, o_ref):
    y = jnp.dot(a_ref[...], w_ref[...], preferred_element_type=F32)
    o_ref[...] = x_ref[...] + gt_ref[...] * y


def _out_proj(a, w, x, gate):
    b, l, k = a.shape
    tm = min(1024, l)
    tn = 512
    return pl.pallas_call(
        _out_proj_body,
        grid=(b, l // tm, D_MODEL // tn),
        in_specs=[pl.BlockSpec((None, tm, k), lambda bi, i, j: (bi, i, 0)),
                  pl.BlockSpec((k, tn), lambda bi, i, j: (0, j)),
                  pl.BlockSpec((None, tm, tn), lambda bi, i, j: (bi, i, j)),
                  pl.BlockSpec((None, 1, tn), lambda bi, i, j: (bi, 0, j))],
        out_specs=pl.BlockSpec((None, tm, tn), lambda bi, i, j: (bi, i, j)),
        out_shape=jax.ShapeDtypeStruct((b, l, D_MODEL), F32),
        compiler_params=_params("parallel", "parallel", "parallel"),
        name="out_proj",
    )(a, w, x, gate)


def _mla_in_body(x_ref, g_ref, sc_ref, sh_ref, wq_ref, wkv_ref, wka_ref, wkb_ref, wz_ref,
                 qn_ref, kvn_ref, cc_ref, ss_ref,
                 cq_out, ckv_out, kr_out, zs_out, h_ref, *, tm, zchunk):
    _modnorm_to(x_ref, g_ref, sc_ref, sh_ref, h_ref, tm, min(tm, 64))
    h = h_ref[...]

    def rms(v, gain):
        ms = jnp.mean(v * v, axis=-1, keepdims=True)
        return (v * lax.rsqrt(ms + EPS) * gain).astype(BF16)

    cq_out[...] = rms(jnp.dot(h, wq_ref[...], preferred_element_type=F32), qn_ref[...])
    ckv_out[...] = rms(jnp.dot(h, wkv_ref[...], preferred_element_type=F32), kvn_ref[...])
    ka = jnp.dot(h, wka_ref[...], preferred_element_type=F32)
    kb = jnp.dot(h, wkb_ref[...], preferred_element_type=F32)
    kr = ka * cc_ref[...] + kb * ss_ref[...]
    kr_out[...] = kr[:, :QK_ROPE_DIM].astype(BF16)
    for c in range(MLA_WIDTH // zchunk):
        z = jnp.dot(h, wz_ref[:, c * zchunk:(c + 1) * zchunk], preferred_element_type=F32)
        zs_out[:, c * zchunk:(c + 1) * zchunk] = _silu(z)


def _mla_in(x, g, scale, shift, wq, wkv, wka, wkb, wz, q_norm, kv_norm, cc, ss):
    b, l, _ = x.shape
    tm = min(256, l)
    const = lambda bi, i: (0, 0)
    rows = lambda bi, i: (bi, i, 0)
    per_b = lambda bi, i: (bi, 0, 0)
    return pl.pallas_call(
        functools.partial(_mla_in_body, tm=tm, zchunk=512),
        grid=(b, l // tm),
        in_specs=[pl.BlockSpec((None, tm, D_MODEL), rows),
                  pl.BlockSpec((1, D_MODEL), const),
                  pl.BlockSpec((None, 1, D_MODEL), per_b),
                  pl.BlockSpec((None, 1, D_MODEL), per_b),
                  pl.BlockSpec((D_MODEL, Q_LORA_RANK), const),
                  pl.BlockSpec((D_MODEL, KV_LORA_RANK), const),
                  pl.BlockSpec((D_MODEL, LANES), const),
                  pl.BlockSpec((D_MODEL, LANES), const),
                  pl.BlockSpec((D_MODEL, MLA_WIDTH), const),
                  pl.BlockSpec((1, Q_LORA_RANK), const),
                  pl.BlockSpec((1, KV_LORA_RANK), const),
                  pl.BlockSpec((tm, LANES), lambda bi, i: (i, 0)),
                  pl.BlockSpec((tm, LANES), lambda bi, i: (i, 0))],
        out_specs=[pl.BlockSpec((None, tm, Q_LORA_RANK), rows),
                   pl.BlockSpec((None, tm, KV_LORA_RANK), rows),
                   pl.BlockSpec((None, tm, QK_ROPE_DIM), rows),
                   pl.BlockSpec((None, tm, MLA_WIDTH), rows)],
        out_shape=[jax.ShapeDtypeStruct((b, l, Q_LORA_RANK), BF16),
                   jax.ShapeDtypeStruct((b, l, KV_LORA_RANK), BF16),
                   jax.ShapeDtypeStruct((b, l, QK_ROPE_DIM), BF16),
                   jax.ShapeDtypeStruct((b, l, MLA_WIDTH), F32)],
        scratch_shapes=[pltpu.VMEM((tm, D_MODEL), BF16)],
        compiler_params=_params("parallel", "parallel"),
        name="mla_in",
    )(x, g, scale, shift, wq, wkv, wka, wkb, wz, q_norm, kv_norm, cc, ss)


def _mla_up_body(cq_ref, ckv_ref, kr_ref, cost_ref, sint_ref, wuqt_ref, wuk_ref, wuvt_ref,
                 qt_out, k_out, vt_out):
    cq = cq_ref[...]
    ckv = ckv_ref[...]
    cost = cost_ref[...]
    sint = sint_ref[...]
    qt = lax.dot_general(wuqt_ref[...], cq, NT_DIMS, preferred_element_type=F32)
    for h in range(N_HEADS):
        r0 = h * QK_DIM
        r1 = r0 + QK_NOPE_DIM
        r2 = r1 + HALF_ROPE
        r3 = r2 + HALF_ROPE
        x1 = qt[r1:r2]
        x2 = qt[r2:r3]
        qt_out[r0:r1, :] = (qt[r0:r1] * Q_PRESCALE).astype(BF16)
        qt_out[r1:r2, :] = ((x1 * cost - x2 * sint) * Q_PRESCALE).astype(BF16)
        qt_out[r2:r3, :] = ((x2 * cost + x1 * sint) * Q_PRESCALE).astype(BF16)
    kn = jnp.dot(ckv, wuk_ref[...], preferred_element_type=F32)
    kr = kr_ref[...]
    for h in range(N_HEADS):
        k_out[h, :, 0:QK_NOPE_DIM] = kn[:, h * QK_NOPE_DIM:(h + 1) * QK_NOPE_DIM].astype(BF16)
        k_out[h, :, QK_NOPE_DIM:QK_DIM] = kr
    vt = lax.dot_general(wuvt_ref[...], ckv, NT_DIMS, preferred_element_type=F32)
    vt_out[...] = vt.astype(BF16)


def _mla_up(cq, ckv, kr, cost, sint, wuqt, wuk, wuvt):
    b, l, _ = cq.shape
    tm = min(256, l)
    const = lambda bi, i: (0, 0)
    rows = lambda bi, i: (bi, i, 0)
    cols = lambda bi, i: (bi, 0, i)
    return pl.pallas_call(
        _mla_up_body,
        grid=(b, l // tm),
        in_specs=[pl.BlockSpec((None, tm, Q_LORA_RANK), rows),
                  pl.BlockSpec((None, tm, KV_LORA_RANK), rows),
                  pl.BlockSpec((None, tm, QK_ROPE_DIM), rows),
                  pl.BlockSpec((HALF_ROPE, tm), lambda bi, i: (0, i)),
                  pl.BlockSpec((HALF_ROPE, tm), lambda bi, i: (0, i)),
                  pl.BlockSpec((N_HEADS * QK_DIM, Q_LORA_RANK), const),
                  pl.BlockSpec((KV_LORA_RANK, N_HEADS * QK_NOPE_DIM), const),
                  pl.BlockSpec((MLA_WIDTH, KV_LORA_RANK), const)],
        out_specs=[pl.BlockSpec((None, N_HEADS * QK_DIM, tm), cols),
                   pl.BlockSpec((None, N_HEADS, tm, QK_DIM), lambda bi, i: (bi, 0, i, 0)),
                   pl.BlockSpec((None, MLA_WIDTH, tm), cols)],
        out_shape=[jax.ShapeDtypeStruct((b, N_HEADS * QK_DIM, l), BF16),
                   jax.ShapeDtypeStruct((b, N_HEADS, l, QK_DIM), BF16),
                   jax.ShapeDtypeStruct((b, MLA_WIDTH, l), BF16)],
        compiler_params=_params("parallel", "parallel"),
        name="mla_up",
    )(cq, ckv, kr, cost, sint, wuqt, wuk, wuvt)


def _attn_body(qt_ref, k_ref, vt_ref, zs_ref, o_ref,
               acc_ref, m_ref, l_ref, s_a, s_b, p_a, p_b, cm_a, cm_b, al_a, al_b, *, nchunks, tk):
    def scores(c, s_ref, cm_ref):
        r = pl.multiple_of(c * tk, tk)
        s = jnp.dot(k_ref[pl.ds(r, tk), :], qt_ref[...], preferred_element_type=F32)
        s_ref[...] = s
        cm_ref[...] = jnp.max(s, axis=0, keepdims=True)

    def softmax(s_ref, cm_ref, p_ref, al_ref):
        m_prev = m_ref[...]
        m_new = jnp.maximum(m_prev, cm_ref[...])
        alpha = jnp.exp2(m_prev - m_new)
        p = jnp.exp2(s_ref[...] - m_new)
        l_ref[...] = alpha * l_ref[...] + jnp.sum(p, axis=0, keepdims=True)
        p_ref[...] = p.astype(BF16)
        al_ref[...] = alpha
        m_ref[...] = m_new

    def values(c, p_ref, al_ref):
        r = pl.multiple_of(c * tk, tk)
        pv = jnp.dot(vt_ref[:, pl.ds(r, tk)], p_ref[...], preferred_element_type=F32)
        acc_ref[...] = al_ref[...] * acc_ref[...] + pv

    m_ref[...] = jnp.full(m_ref.shape, -jnp.inf, F32)
    l_ref[...] = jnp.zeros(l_ref.shape, F32)
    acc_ref[...] = jnp.zeros(acc_ref.shape, F32)
    p_b[...] = jnp.zeros(p_b.shape, BF16)
    al_b[...] = jnp.ones(al_b.shape, F32)
    scores(0, s_a, cm_a)

    def body(j, carry):
        c0 = 2 * j
        scores(c0 + 1, s_b, cm_b)
        softmax(s_a, cm_a, p_a, al_a)
        values(jnp.maximum(c0 - 1, 0), p_b, al_b)
        scores(jnp.minimum(c0 + 2, nchunks - 1), s_a, cm_a)
        softmax(s_b, cm_b, p_b, al_b)
        values(c0, p_a, al_a)
        return carry

    lax.fori_loop(0, nchunks // 2, body, 0)
    values(nchunks - 1, p_b, al_b)
    o = acc_ref[...] / l_ref[...]
    o_ref[...] = (o.T * zs_ref[...]).astype(BF16)


def _attention(qt, k, vt, zs):
    b, _, l = qt.shape
    tq = min(512, l)
    tk = min(512, l // 2)
    nchunks = l // tk
    assert nchunks % 2 == 0
    stat = pltpu.VMEM((1, tq), F32)
    return pl.pallas_call(
        functools.partial(_attn_body, nchunks=nchunks, tk=tk),
        grid=(b, N_HEADS, l // tq),
        in_specs=[pl.BlockSpec((None, QK_DIM, tq), lambda bi, h, qi: (bi, h, qi)),
                  pl.BlockSpec((None, None, l, QK_DIM), lambda bi, h, qi: (bi, h, 0, 0)),
                  pl.BlockSpec((None, V_HEAD_DIM, l), lambda bi, h, qi: (bi, h, 0)),
                  pl.BlockSpec((None, tq, V_HEAD_DIM), lambda bi, h, qi: (bi, qi, h))],
        out_specs=pl.BlockSpec((None, tq, V_HEAD_DIM), lambda bi, h, qi: (bi, qi, h)),
        out_shape=jax.ShapeDtypeStruct((b, l, MLA_WIDTH), BF16),
        scratch_shapes=[pltpu.VMEM((V_HEAD_DIM, tq), F32), stat, stat,
                        pltpu.VMEM((tk, tq), F32), pltpu.VMEM((tk, tq), F32),
                        pltpu.VMEM((tk, tq), BF16), pltpu.VMEM((tk, tq), BF16),
                        stat, stat, stat, stat],
        compiler_params=_params("parallel", "parallel", "parallel"),
        name="mla_attn",
    )(qt, k, vt, zs)


def _final_norm_body(x_ref, g_ref, o_ref):
    x = x_ref[...]
    ms = jnp.mean(x * x, axis=-1, keepdims=True)
    o_ref[...] = x * lax.rsqrt(ms + EPS) * g_ref[...]


def _final_norm(x, g):
    b, l, _ = x.shape
    tm = min(256, l)
    return pl.pallas_call(
        _final_norm_body,
        grid=(b, l // tm),
        in_specs=[pl.BlockSpec((None, tm, D_MODEL), lambda bi, i: (bi, i, 0)),
                  pl.BlockSpec((1, D_MODEL), lambda bi, i: (0, 0))],
        out_specs=pl.BlockSpec((None, tm, D_MODEL), lambda bi, i: (bi, i, 0)),
        out_shape=jax.ShapeDtypeStruct(x.shape, F32),
        compiler_params=_params("parallel", "parallel"),
        name="final_norm",
    )(x, g)


def _rope_tables(length):
    inv = 1.0 / (ROPE_THETA ** (jnp.arange(0, QK_ROPE_DIM, 2, dtype=F32) / QK_ROPE_DIM))
    ang = jnp.arange(length, dtype=F32)[:, None] * inv[None, :]
    return jnp.cos(ang), jnp.sin(ang)


def _prep_weights(conv_w_in, conv_w_out, mla_w_in, mla_w_uq, mla_w_ukv, mla_w_out):
    q0, q1, q2 = Q_LORA_RANK, Q_LORA_RANK + KV_LORA_RANK, Q_LORA_RANK + KV_LORA_RANK + QK_ROPE_DIM
    pad = ((0, 0), (0, 0), (0, LANES - QK_ROPE_DIM))
    wkr = mla_w_in[:, :, q1:q2]
    wkr_swapped = jnp.concatenate([wkr[..., HALF_ROPE:], wkr[..., :HALF_ROPE]], axis=-1)
    n_mla = mla_w_ukv.shape[0]
    wukv = mla_w_ukv.reshape(n_mla, KV_LORA_RANK, N_HEADS, QK_NOPE_DIM + V_HEAD_DIM)
    return dict(
        conv_w_in=conv_w_in.astype(BF16),
        conv_w_out=conv_w_out.astype(BF16),
        wq=mla_w_in[:, :, :q0].astype(BF16),
        wkv=mla_w_in[:, :, q0:q1].astype(BF16),
        wka=jnp.pad(wkr, pad).astype(BF16),
        wkb=jnp.pad(wkr_swapped, pad).astype(BF16),
        wz=mla_w_in[:, :, q2:].astype(BF16),
        wuqt=jnp.swapaxes(mla_w_uq, 1, 2).astype(BF16),
        wuk=wukv[..., :QK_NOPE_DIM].reshape(n_mla, KV_LORA_RANK, N_HEADS * QK_NOPE_DIM).astype(BF16),
        wuvt=jnp.swapaxes(wukv[..., QK_NOPE_DIM:].reshape(n_mla, KV_LORA_RANK, MLA_WIDTH), 1, 2).astype(BF16),
        mla_w_out=mla_w_out.astype(BF16),
    )


def _trunk(x, mod, norm_g, wts, conv_dw_w, conv_dw_b, conv_ln_g, conv_ln_b,
           mla_q_norm, mla_kv_norm, final_g):
    length = x.shape[1]
    cos, sin = _rope_tables(length)
    lane_pad = ((0, 0), (0, LANES - QK_ROPE_DIM))
    cc = jnp.pad(jnp.concatenate([cos, cos], axis=-1), lane_pad)
    ss = jnp.pad(jnp.concatenate([-sin, sin], axis=-1), lane_pad)
    cost, sint = cos.T, sin.T
    for i in range(DEPTH):
        shift = mod[i, :, 0][:, None, :]
        scale = mod[i, :, 1][:, None, :]
        gate = mod[i, :, 2][:, None, :]
        g = norm_g[i][None, :]
        j = i // 2
        if i % 2 == 0:
            y, zs = _conv_in(x, g, scale, shift, wts["conv_w_in"][j])
            yc = _dwconv(y, zs, conv_dw_w[j], conv_dw_b[j], conv_ln_g[j], conv_ln_b[j])
            x = _out_proj(yc, wts["conv_w_out"][j], x, gate)
        else:
            cq, ckv, kr, zs = _mla_in(x, g, scale, shift, wts["wq"][j], wts["wkv"][j], wts["wka"][j],
                                      wts["wkb"][j], wts["wz"][j], mla_q_norm[j][None, :],
                                      mla_kv_norm[j][None, :], cc, ss)
            qt, k, vt = _mla_up(cq, ckv, kr, cost, sint, wts["wuqt"][j], wts["wuk"][j], wts["wuvt"][j])
            og = _attention(qt, k, vt, zs)
            x = _out_proj(og, wts["mla_w_out"][j], x, gate)
    return _final_norm(x, final_g[None, :])


def kernel(x_prompt, x_sample, c_prompt, c_sample, norm_g, ada_w, ada_b, conv_w_in, conv_dw_w, conv_dw_b,
           conv_ln_g, conv_ln_b, conv_w_out, mla_w_in, mla_q_norm, mla_kv_norm, mla_w_uq, mla_w_ukv,
           mla_w_out, final_g):
    nb_p, nb_s = c_prompt.shape[0], c_sample.shape[0]
    c_all = jnp.concatenate([c_prompt, c_sample, jnp.zeros((MOD_ROWS - nb_p - nb_s, D_MODEL), F32)], axis=0)
    mod = _ada_mod(c_all, ada_w, ada_b).reshape(DEPTH, MOD_ROWS, 3, D_MODEL)
    wts = _prep_weights(conv_w_in, conv_w_out, mla_w_in, mla_w_uq, mla_w_ukv, mla_w_out)
    rest = (norm_g, wts, conv_dw_w, conv_dw_b, conv_ln_g, conv_ln_b, mla_q_norm, mla_kv_norm, final_g)
    y_prompt = _trunk(x_prompt, mod[:, :nb_p], *rest)
    y_sample = _trunk(x_sample, mod[:, nb_p:nb_p + nb_s], *rest)
    return (y_prompt, y_sample)
```

```python
import functools
import math

import jax
import jax.numpy as jnp
from jax import lax
from jax.experimental import pallas as pl
from jax.experimental.pallas import tpu as pltpu

F32 = jnp.float32
BF16 = jnp.bfloat16

D_MODEL = 2048
DEPTH = 4
CONV_WIDTH = 4096
CONV_KERNEL = 31
CONV_HALO = 16
N_HEADS = 16
QK_NOPE_DIM = 128
QK_ROPE_DIM = 64
HALF_ROPE = QK_ROPE_DIM // 2
QK_DIM = QK_NOPE_DIM + QK_ROPE_DIM
V_HEAD_DIM = 128
Q_LORA_RANK = 512
KV_LORA_RANK = 512
MLA_WIDTH = N_HEADS * V_HEAD_DIM
ROPE_THETA = 10000.0
EPS = 1e-6
LANES = 128
BF16_ROWS = 16
MOD_ROWS = 8
VMEM_LIMIT_BYTES = 56 * 1024 * 1024
Q_PRESCALE = math.log2(math.e) / math.sqrt(QK_DIM)
NT_DIMS = (((1,), (1,)), ((), ()))
ATTN_UNROLL = 4


def _params(*sem):
    return pltpu.CompilerParams(dimension_semantics=sem, vmem_limit_bytes=VMEM_LIMIT_BYTES)


def _silu(v):
    return v * jax.nn.sigmoid(v)


def _ada_body(c_ref, w_ref, b_ref, o_ref):
    c = c_ref[...]
    ca = _silu(c).astype(BF16)
    o_ref[0] = jnp.dot(ca, w_ref[0].astype(BF16), preferred_element_type=F32) + b_ref[0]


def _ada_mod(c_all, ada_w, ada_b):
    n = 3 * D_MODEL
    tn = 1024
    return pl.pallas_call(
        _ada_body,
        grid=(DEPTH, n // tn),
        in_specs=[pl.BlockSpec((MOD_ROWS, D_MODEL), lambda i, j: (0, 0)),
                  pl.BlockSpec((1, D_MODEL, tn), lambda i, j: (i, 0, j)),
                  pl.BlockSpec((1, 1, tn), lambda i, j: (i, 0, j))],
        out_specs=pl.BlockSpec((1, MOD_ROWS, tn), lambda i, j: (i, 0, j)),
        out_shape=jax.ShapeDtypeStruct((DEPTH, MOD_ROWS, n), F32),
        compiler_params=_params("parallel", "parallel"),
        name="ada_mod",
    )(c_all, ada_w, ada_b.reshape(DEPTH, 1, n))


def _modnorm_to(x_ref, g_ref, sc_ref, sh_ref, h_ref, tm, chunk):
    g = g_ref[...]
    sc = 1.0 + sc_ref[...]
    sh = sh_ref[...]

    def body(c, carry):
        r = pl.multiple_of(c * chunk, chunk)
        x = x_ref[pl.ds(r, chunk), :]
        ms = jnp.mean(x * x, axis=-1, keepdims=True)
        y = x * lax.rsqrt(ms + EPS) * g
        h_ref[pl.ds(r, chunk), :] = (y * sc + sh).astype(BF16)
        return carry

    lax.fori_loop(0, tm // chunk, body, 0)


def _conv_in_body(x_ref, g_ref, sc_ref, sh_ref, wa_ref, wg_ref, wz_ref, y_ref, z_ref, h_ref, *, tm):
    @pl.when(pl.program_id(2) == 0)
    def _():
        _modnorm_to(x_ref, g_ref, sc_ref, sh_ref, h_ref, tm, min(tm, 64))

    h = h_ref[...]
    a = jnp.dot(h, wa_ref[...], preferred_element_type=F32)
    gl = jnp.dot(h, wg_ref[...], preferred_element_type=F32)
    y_ref[...] = (a * jax.nn.sigmoid(gl)).astype(BF16)
    z = jnp.dot(h, wz_ref[...], preferred_element_type=F32)
    z_ref[...] = _silu(z)


def _conv_in(x, g, scale, shift, w_in):
    b, l, _ = x.shape
    c = CONV_WIDTH
    tm = min(1024, l)
    tn = 512
    nj = c // tn
    return pl.pallas_call(
        functools.partial(_conv_in_body, tm=tm),
        grid=(b, l // tm, nj),
        in_specs=[pl.BlockSpec((None, tm, D_MODEL), lambda bi, i, j: (bi, i, 0)),
                  pl.BlockSpec((1, D_MODEL), lambda bi, i, j: (0, 0)),
                  pl.BlockSpec((None, 1, D_MODEL), lambda bi, i, j: (bi, 0, 0)),
                  pl.BlockSpec((None, 1, D_MODEL), lambda bi, i, j: (bi, 0, 0)),
                  pl.BlockSpec((D_MODEL, tn), lambda bi, i, j: (0, j)),
                  pl.BlockSpec((D_MODEL, tn), lambda bi, i, j: (0, j + nj)),
                  pl.BlockSpec((D_MODEL, tn), lambda bi, i, j: (0, j + 2 * nj))],
        out_specs=[pl.BlockSpec((None, tm, tn), lambda bi, i, j: (bi, i, j)),
                   pl.BlockSpec((None, tm, tn), lambda bi, i, j: (bi, i, j))],
        out_shape=[jax.ShapeDtypeStruct((b, l, c), BF16), jax.ShapeDtypeStruct((b, l, c), F32)],
        scratch_shapes=[pltpu.VMEM((tm, D_MODEL), BF16)],
        compiler_params=_params("parallel", "parallel", "arbitrary"),
        name="conv_in",
    )(x, g, scale, shift, w_in, w_in, w_in)


def _dwconv_body(yp_ref, ym_ref, yn_ref, zs_ref, w_ref, b_ref, lg_ref, lb_ref, o_ref,
                 xbuf, pbuf, shbuf, wbuf, cbuf, *, tl, ln_rows):
    i = pl.program_id(1)
    n = pl.num_programs(1)
    halo = CONV_HALO
    xbuf[0:halo, :] = jnp.where(i > 0, yp_ref[...].astype(F32), 0.0)
    xbuf[halo:halo + tl, :] = ym_ref[...].astype(F32)
    xbuf[halo + tl:2 * halo + tl, :] = jnp.where(i < n - 1, yn_ref[...].astype(F32), 0.0)
    xbuf[2 * halo + tl:, :] = jnp.zeros((xbuf.shape[0] - 2 * halo - tl, CONV_WIDTH), F32)

    first = halo - CONV_KERNEL // 2
    sh_rows = tl + 2 * BF16_ROWS
    strip = BF16_ROWS

    def cb_body(cb, carry):
        lanes = pl.ds(pl.multiple_of(cb * LANES, LANES), LANES)
        for q in range(2):
            pbuf[q, :, :] = pltpu.bitcast(xbuf[pl.ds(q, sh_rows), lanes].astype(BF16), jnp.uint32)
        for r in range(BF16_ROWS):
            shbuf[r, :, :] = pltpu.bitcast(pbuf[r % 2, pl.ds(r // 2, (tl + BF16_ROWS) // 2), :], BF16)
        for k in range(CONV_KERNEL):
            wbuf[k, :, :] = w_ref[k, :, lanes]
        bias = b_ref[:, lanes]
        for s in range(tl // strip):
            acc = None
            for k in range(CONV_KERNEL):
                a16, r = divmod(first + k, BF16_ROWS)
                win = shbuf[r, pl.ds(s * strip + BF16_ROWS * a16, strip), :]
                term = win.astype(F32) * wbuf[k].astype(F32)
                acc = term if acc is None else acc + term
            cbuf[pl.ds(s * strip, strip), lanes] = acc + bias
        return carry

    lax.fori_loop(0, CONV_WIDTH // LANES, cb_body, 0)

    lg = lg_ref[...]
    lb = lb_ref[...]

    def ln_body(t, carry):
        rows = pl.ds(pl.multiple_of(t * ln_rows, ln_rows), ln_rows)
        mu = jnp.mean(cbuf[rows, :], axis=-1, keepdims=True)
        vc = cbuf[rows, :] - mu
        var = jnp.mean(vc * vc, axis=-1, keepdims=True)
        yn = (cbuf[rows, :] - mu) * lax.rsqrt(var + EPS) * lg + lb
        o_ref[rows, :] = (_silu(yn) * zs_ref[rows, :]).astype(BF16)
        return carry

    lax.fori_loop(0, tl // ln_rows, ln_body, 0)


def _dwconv(y, zs, dw_w, dw_b, ln_g, ln_b):
    b, l, c = y.shape
    tl = min(256, l)
    hb = tl // CONV_HALO
    nhb = l // CONV_HALO
    body = functools.partial(_dwconv_body, tl=tl, ln_rows=min(64, tl))
    row = lambda bi, i: (0, 0)
    w_tiles = jnp.broadcast_to(dw_w.astype(BF16)[:, None, :], (CONV_KERNEL, BF16_ROWS, c))
    return pl.pallas_call(
        body,
        grid=(b, l // tl),
        in_specs=[pl.BlockSpec((None, CONV_HALO, c), lambda bi, i: (bi, jnp.maximum(i * hb - 1, 0), 0)),
                  pl.BlockSpec((None, tl, c), lambda bi, i: (bi, i, 0)),
                  pl.BlockSpec((None, CONV_HALO, c), lambda bi, i: (bi, jnp.minimum((i + 1) * hb, nhb - 1), 0)),
                  pl.BlockSpec((None, tl, c), lambda bi, i: (bi, i, 0)),
                  pl.BlockSpec((CONV_KERNEL, BF16_ROWS, c), lambda bi, i: (0, 0, 0)),
                  pl.BlockSpec((1, c), row),
                  pl.BlockSpec((1, c), row),
                  pl.BlockSpec((1, c), row)],
        out_specs=pl.BlockSpec((None, tl, c), lambda bi, i: (bi, i, 0)),
        out_shape=jax.ShapeDtypeStruct((b, l, c), BF16),
        scratch_shapes=[pltpu.VMEM((tl + 2 * CONV_HALO + 8, c), F32),
                        pltpu.VMEM((2, (tl + 2 * BF16_ROWS) // 2, LANES), jnp.uint32),
                        pltpu.VMEM((BF16_ROWS, tl + BF16_ROWS, LANES), BF16),
                        pltpu.VMEM((CONV_KERNEL, BF16_ROWS, LANES), BF16),
                        pltpu.VMEM((tl, c), F32)],
        compiler_params=_params("parallel", "parallel"),
        name="dwconv_ln",
    )(y, y, y, zs, w_tiles, dw_b.reshape(1, c), ln_g.reshape(1, c), ln_b.reshape(1, c))


def _out_proj_body(a_ref, w_ref, x_ref, gt_ref, o_ref):
    y = jnp.dot(a_ref[...], w_ref[...], preferred_element_type=F32)
    o_ref[...] = x_ref[...] + gt_ref[...] * y


def _out_proj(a, w, x, gate):
    b, l, k = a.shape
    tm = min(1024, l)
    tn = 512
    return pl.pallas_call(
        _out_proj_body,
        grid=(b, l // tm, D_MODEL // tn),
        in_specs=[pl.BlockSpec((None, tm, k), lambda bi, i, j: (bi, i, 0)),
                  pl.BlockSpec((k, tn), lambda bi, i, j: (0, j)),
                  pl.BlockSpec((None, tm, tn), lambda bi, i, j: (bi, i, j)),
                  pl.BlockSpec((None, 1, tn), lambda bi, i, j: (bi, 0, j))],
        out_specs=pl.BlockSpec((None, tm, tn), lambda bi, i, j: (bi, i, j)),
        out_shape=jax.ShapeDtypeStruct((b, l, D_MODEL), F32),
        compiler_params=_params("parallel", "parallel", "parallel"),
        name="out_proj",
    )(a, w, x, gate)


def _mla_in_body(x_ref, g_ref, sc_ref, sh_ref, wq_ref, wkv_ref, wka_ref, wkb_ref, wz_ref,
                 qn_ref, kvn_ref, cc_ref, ss_ref,
                 cq_out, ckv_out, kr_out, zs_out, h_ref, *, tm, zchunk):
    _modnorm_to(x_ref, g_ref, sc_ref, sh_ref, h_ref, tm, min(tm, 64))
    h = h_ref[...]

    def rms(v, gain):
        ms = jnp.mean(v * v, axis=-1, keepdims=True)
        return (v * lax.rsqrt(ms + EPS) * gain).astype(BF16)

    cq_out[...] = rms(jnp.dot(h, wq_ref[...], preferred_element_type=F32), qn_ref[...])
    ckv_out[...] = rms(jnp.dot(h, wkv_ref[...], preferred_element_type=F32), kvn_ref[...])
    ka = jnp.dot(h, wka_ref[...], preferred_element_type=F32)
    kb = jnp.dot(h, wkb_ref[...], preferred_element_type=F32)
    kr = ka * cc_ref[...] + kb * ss_ref[...]
    kr_out[...] = kr[:, :QK_ROPE_DIM].astype(BF16)
    for c in range(MLA_WIDTH // zchunk):
        z = jnp.dot(h, wz_ref[:, c * zchunk:(c + 1) * zchunk], preferred_element_type=F32)
        zs_out[:, c * zchunk:(c + 1) * zchunk] = _silu(z)


def _mla_in(x, g, scale, shift, wq, wkv, wka, wkb, wz, q_norm, kv_norm, cc, ss):
    b, l, _ = x.shape
    tm = min(256, l)
    const = lambda bi, i: (0, 0)
    rows = lambda bi, i: (bi, i, 0)
    per_b = lambda bi, i: (bi, 0, 0)
    return pl.pallas_call(
        functools.partial(_mla_in_body, tm=tm, zchunk=512),
        grid=(b, l // tm),
        in_specs=[pl.BlockSpec((None, tm, D_MODEL), rows),
                  pl.BlockSpec((1, D_MODEL), const),
                  pl.BlockSpec((None, 1, D_MODEL), per_b),
                  pl.BlockSpec((None, 1, D_MODEL), per_b),
                  pl.BlockSpec((D_MODEL, Q_LORA_RANK), const),
                  pl.BlockSpec((D_MODEL, KV_LORA_RANK), const),
                  pl.BlockSpec((D_MODEL, LANES), const),
                  pl.BlockSpec((D_MODEL, LANES), const),
                  pl.BlockSpec((D_MODEL, MLA_WIDTH), const),
                  pl.BlockSpec((1, Q_LORA_RANK), const),
                  pl.BlockSpec((1, KV_LORA_RANK), const),
                  pl.BlockSpec((tm, LANES), lambda bi, i: (i, 0)),
                  pl.BlockSpec((tm, LANES), lambda bi, i: (i, 0))],
        out_specs=[pl.BlockSpec((None, tm, Q_LORA_RANK), rows),
                   pl.BlockSpec((None, tm, KV_LORA_RANK), rows),
                   pl.BlockSpec((None, tm, QK_ROPE_DIM), rows),
                   pl.BlockSpec((None, tm, MLA_WIDTH), rows)],
        out_shape=[jax.ShapeDtypeStruct((b, l, Q_LORA_RANK), BF16),
                   jax.ShapeDtypeStruct((b, l, KV_LORA_RANK), BF16),
                   jax.ShapeDtypeStruct((b, l, QK_ROPE_DIM), BF16),
                   jax.ShapeDtypeStruct((b, l, MLA_WIDTH), F32)],
        scratch_shapes=[pltpu.VMEM((tm, D_MODEL), BF16)],
        compiler_params=_params("parallel", "parallel"),
        name="mla_in",
    )(x, g, scale, shift, wq, wkv, wka, wkb, wz, q_norm, kv_norm, cc, ss)


def _mla_up_body(cq_ref, ckv_ref, kr_ref, cost_ref, sint_ref, wuqt_ref, wuk_ref, wuvt_ref,
                 qt_out, k_out, vt_out):
    cq = cq_ref[...]
    ckv = ckv_ref[...]
    cost = cost_ref[...]
    sint = sint_ref[...]
    qt = lax.dot_general(wuqt_ref[...], cq, NT_DIMS, preferred_element_type=F32)
    for h in range(N_HEADS):
        r0 = h * QK_DIM
        r1 = r0 + QK_NOPE_DIM
        r2 = r1 + HALF_ROPE
        r3 = r2 + HALF_ROPE
        x1 = qt[r1:r2]
        x2 = qt[r2:r3]
        qt_out[r0:r1, :] = (qt[r0:r1] * Q_PRESCALE).astype(BF16)
        qt_out[r1:r2, :] = ((x1 * cost - x2 * sint) * Q_PRESCALE).astype(BF16)
        qt_out[r2:r3, :] = ((x2 * cost + x1 * sint) * Q_PRESCALE).astype(BF16)
    kn = jnp.dot(ckv, wuk_ref[...], preferred_element_type=F32)
    kr = kr_ref[...]
    for h in range(N_HEADS):
        k_out[h, :, 0:QK_NOPE_DIM] = kn[:, h * QK_NOPE_DIM:(h + 1) * QK_NOPE_DIM].astype(BF16)
        k_out[h, :, QK_NOPE_DIM:QK_DIM] = kr
    vt = lax.dot_general(wuvt_ref[...], ckv, NT_DIMS, preferred_element_type=F32)
    vt_out[...] = vt.astype(BF16)


def _mla_up(cq, ckv, kr, cost, sint, wuqt, wuk, wuvt):
    b, l, _ = cq.shape
    tm = min(256, l)
    const = lambda bi, i: (0, 0)
    rows = lambda bi, i: (bi, i, 0)
    cols = lambda bi, i: (bi, 0, i)
    return pl.pallas_call(
        _mla_up_body,
        grid=(b, l // tm),
        in_specs=[pl.BlockSpec((None, tm, Q_LORA_RANK), rows),
                  pl.BlockSpec((None, tm, KV_LORA_RANK), rows),
                  pl.BlockSpec((None, tm, QK_ROPE_DIM), rows),
                  pl.BlockSpec((HALF_ROPE, tm), lambda bi, i: (0, i)),
                  pl.BlockSpec((HALF_ROPE, tm), lambda bi, i: (0, i)),
                  pl.BlockSpec((N_HEADS * QK_DIM, Q_LORA_RANK), const),
                  pl.BlockSpec((KV_LORA_RANK, N_HEADS * QK_NOPE_DIM), const),
                  pl.BlockSpec((MLA_WIDTH, KV_LORA_RANK), const)],
        out_specs=[pl.BlockSpec((None, N_HEADS * QK_DIM, tm), cols),
                   pl.BlockSpec((None, N_HEADS, tm, QK_DIM), lambda bi, i: (bi, 0, i, 0)),
                   pl.BlockSpec((None, MLA_WIDTH, tm), cols)],
        out_shape=[jax.ShapeDtypeStruct((b, N_HEADS * QK_DIM, l), BF16),
                   jax.ShapeDtypeStruct((b, N_HEADS, l, QK_DIM), BF16),
                   jax.ShapeDtypeStruct((b, MLA_WIDTH, l), BF16)],
        compiler_params=_params("parallel", "parallel"),
        name="mla_up",
    )(cq, ckv, kr, cost, sint, wuqt, wuk, wuvt)


def _attn_body(qt_ref, k_ref, vt_ref, zs_ref, o_ref, acc_ref, m_ref, l_ref, *stage_refs, nchunks, tk):
    bufs = tuple(stage_refs[4 * u:4 * u + 4] for u in range(ATTN_UNROLL))

    def scores(c, s_ref, cm_ref):
        r = pl.multiple_of(c * tk, tk)
        s = jnp.dot(k_ref[pl.ds(r, tk), :], qt_ref[...], preferred_element_type=F32)
        s_ref[...] = s
        cm_ref[...] = jnp.max(s, axis=0, keepdims=True)

    def softmax(s_ref, cm_ref, p_ref, al_ref):
        m_prev = m_ref[...]
        m_new = jnp.maximum(m_prev, cm_ref[...])
        alpha = jnp.exp2(m_prev - m_new)
        p = jnp.exp2(s_ref[...] - m_new)
        l_ref[...] = alpha * l_ref[...] + jnp.sum(p, axis=0, keepdims=True)
        p_ref[...] = p.astype(BF16)
        al_ref[...] = alpha
        m_ref[...] = m_new

    def values(c, p_ref, al_ref):
        r = pl.multiple_of(c * tk, tk)
        pv = jnp.dot(vt_ref[:, pl.ds(r, tk)], p_ref[...], preferred_element_type=F32)
        acc_ref[...] = al_ref[...] * acc_ref[...] + pv

    m_ref[...] = jnp.full(m_ref.shape, -jnp.inf, F32)
    l_ref[...] = jnp.zeros(l_ref.shape, F32)
    acc_ref[...] = jnp.zeros(acc_ref.shape, F32)
    last = ATTN_UNROLL - 1
    bufs[last][2][...] = jnp.zeros(bufs[last][2].shape, BF16)
    bufs[last][3][...] = jnp.ones(bufs[last][3].shape, F32)
    scores(0, bufs[0][0], bufs[0][1])

    def body(j, carry):
        c0 = ATTN_UNROLL * j
        for u in range(ATTN_UNROLL):
            s_cur, cm_cur, p_cur, al_cur = bufs[u]
            s_nxt, cm_nxt, _, _ = bufs[(u + 1) % ATTN_UNROLL]
            _, _, p_prv, al_prv = bufs[(u - 1) % ATTN_UNROLL]
            softmax(s_cur, cm_cur, p_cur, al_cur)
            values(jnp.maximum(c0 + u - 1, 0), p_prv, al_prv)
            scores(jnp.minimum(c0 + u + 1, nchunks - 1), s_nxt, cm_nxt)
        return carry

    lax.fori_loop(0, nchunks // ATTN_UNROLL, body, 0)
    values(nchunks - 1, bufs[last][2], bufs[last][3])
    o = acc_ref[...] / l_ref[...]
    o_ref[...] = (o.T * zs_ref[...]).astype(BF16)


def _attention(qt, k, vt, zs):
    b, _, l = qt.shape
    tq = min(512, l)
    tk = min(1024, l // ATTN_UNROLL)
    nchunks = l // tk
    assert nchunks % ATTN_UNROLL == 0
    stat = pltpu.VMEM((1, tq), F32)
    return pl.pallas_call(
        functools.partial(_attn_body, nchunks=nchunks, tk=tk),
        grid=(b, N_HEADS, l // tq),
        in_specs=[pl.BlockSpec((None, QK_DIM, tq), lambda bi, h, qi: (bi, h, qi)),
                  pl.BlockSpec((None, None, l, QK_DIM), lambda bi, h, qi: (bi, h, 0, 0)),
                  pl.BlockSpec((None, V_HEAD_DIM, l), lambda bi, h, qi: (bi, h, 0)),
                  pl.BlockSpec((None, tq, V_HEAD_DIM), lambda bi, h, qi: (bi, qi, h))],
        out_specs=pl.BlockSpec((None, tq, V_HEAD_DIM), lambda bi, h, qi: (bi, qi, h)),
        out_shape=jax.ShapeDtypeStruct((b, l, MLA_WIDTH), BF16),
        scratch_shapes=[pltpu.VMEM((V_HEAD_DIM, tq), F32), stat, stat]
        + [pltpu.VMEM((tk, tq), F32), stat, pltpu.VMEM((tk, tq), BF16), stat] * ATTN_UNROLL,
        compiler_params=_params("parallel", "parallel", "parallel"),
        name="mla_attn",
    )(qt, k, vt, zs)


def _final_norm_body(x_ref, g_ref, o_ref):
    x = x_ref[...]
    ms = jnp.mean(x * x, axis=-1, keepdims=True)
    o_ref[...] = x * lax.rsqrt(ms + EPS) * g_ref[...]


def _final_norm(x, g):
    b, l, _ = x.shape
    tm = min(256, l)
    return pl.pallas_call(
        _final_norm_body,
        grid=(b, l // tm),
        in_specs=[pl.BlockSpec((None, tm, D_MODEL), lambda bi, i: (bi, i, 0)),
                  pl.BlockSpec((1, D_MODEL), lambda bi, i: (0, 0))],
        out_specs=pl.BlockSpec((None, tm, D_MODEL), lambda bi, i: (bi, i, 0)),
        out_shape=jax.ShapeDtypeStruct(x.shape, F32),
        compiler_params=_params("parallel", "parallel"),
        name="final_norm",
    )(x, g)


def _rope_tables(length):
    inv = 1.0 / (ROPE_THETA ** (jnp.arange(0, QK_ROPE_DIM, 2, dtype=F32) / QK_ROPE_DIM))
    ang = jnp.arange(length, dtype=F32)[:, None] * inv[None, :]
    return jnp.cos(ang), jnp.sin(ang)


def _prep_weights(conv_w_in, conv_w_out, mla_w_in, mla_w_uq, mla_w_ukv, mla_w_out):
    q0, q1, q2 = Q_LORA_RANK, Q_LORA_RANK + KV_LORA_RANK, Q_LORA_RANK + KV_LORA_RANK + QK_ROPE_DIM
    pad = ((0, 0), (0, 0), (0, LANES - QK_ROPE_DIM))
    wkr = mla_w_in[:, :, q1:q2]
    wkr_swapped = jnp.concatenate([wkr[..., HALF_ROPE:], wkr[..., :HALF_ROPE]], axis=-1)
    n_mla = mla_w_ukv.shape[0]
    wukv = mla_w_ukv.reshape(n_mla, KV_LORA_RANK, N_HEADS, QK_NOPE_DIM + V_HEAD_DIM)
    return dict(
        conv_w_in=conv_w_in.astype(BF16),
        conv_w_out=conv_w_out.astype(BF16),
        wq=mla_w_in[:, :, :q0].astype(BF16),
        wkv=mla_w_in[:, :, q0:q1].astype(BF16),
        wka=jnp.pad(wkr, pad).astype(BF16),
        wkb=jnp.pad(wkr_swapped, pad).astype(BF16),
        wz=mla_w_in[:, :, q2:].astype(BF16),
        wuqt=jnp.swapaxes(mla_w_uq, 1, 2).astype(BF16),
        wuk=wukv[..., :QK_NOPE_DIM].reshape(n_mla, KV_LORA_RANK, N_HEADS * QK_NOPE_DIM).astype(BF16),
        wuvt=jnp.swapaxes(wukv[..., QK_NOPE_DIM:].reshape(n_mla, KV_LORA_RANK, MLA_WIDTH), 1, 2).astype(BF16),
        mla_w_out=mla_w_out.astype(BF16),
    )


def _trunk(x, mod, norm_g, wts, conv_dw_w, conv_dw_b, conv_ln_g, conv_ln_b,
           mla_q_norm, mla_kv_norm, final_g):
    length = x.shape[1]
    cos, sin = _rope_tables(length)
    lane_pad = ((0, 0), (0, LANES - QK_ROPE_DIM))
    cc = jnp.pad(jnp.concatenate([cos, cos], axis=-1), lane_pad)
    ss = jnp.pad(jnp.concatenate([-sin, sin], axis=-1), lane_pad)
    cost, sint = cos.T, sin.T
    for i in range(DEPTH):
        shift = mod[i, :, 0][:, None, :]
        scale = mod[i, :, 1][:, None, :]
        gate = mod[i, :, 2][:, None, :]
        g = norm_g[i][None, :]
        j = i // 2
        if i % 2 == 0:
            y, zs = _conv_in(x, g, scale, shift, wts["conv_w_in"][j])
            yc = _dwconv(y, zs, conv_dw_w[j], conv_dw_b[j], conv_ln_g[j], conv_ln_b[j])
            x = _out_proj(yc, wts["conv_w_out"][j], x, gate)
        else:
            cq, ckv, kr, zs = _mla_in(x, g, scale, shift, wts["wq"][j], wts["wkv"][j], wts["wka"][j],
                                      wts["wkb"][j], wts["wz"][j], mla_q_norm[j][None, :],
                                      mla_kv_norm[j][None, :], cc, ss)
            qt, k, vt = _mla_up(cq, ckv, kr, cost, sint, wts["wuqt"][j], wts["wuk"][j], wts["wuvt"][j])
            og = _attention(qt, k, vt, zs)
            x = _out_proj(og, wts["mla_w_out"][j], x, gate)
    return _final_norm(x, final_g[None, :])


def kernel(x_prompt, x_sample, c_prompt, c_sample, norm_g, ada_w, ada_b, conv_w_in, conv_dw_w, conv_dw_b,
           conv_ln_g, conv_ln_b, conv_w_out, mla_w_in, mla_q_norm, mla_kv_norm, mla_w_uq, mla_w_ukv,
           mla_w_out, final_g):
    nb_p, nb_s = c_prompt.shape[0], c_sample.shape[0]
    c_all = jnp.concatenate([c_prompt, c_sample, jnp.zeros((MOD_ROWS - nb_p - nb_s, D_MODEL), F32)], axis=0)
    mod = _ada_mod(c_all, ada_w, ada_b).reshape(DEPTH, MOD_ROWS, 3, D_MODEL)
    wts = _prep_weights(conv_w_in, conv_w_out, mla_w_in, mla_w_uq, mla_w_ukv, mla_w_out)
    rest = (norm_g, wts, conv_dw_w, conv_dw_b, conv_ln_g, conv_ln_b, mla_q_norm, mla_kv_norm, final_g)
    y_prompt = _trunk(x_prompt, mod[:, :nb_p], *rest)
    y_sample = _trunk(x_sample, mod[:, nb_p:nb_p + nb_s], *rest)
    return (y_prompt, y_sample)
```

```python
import functools
import math

import jax
import jax.numpy as jnp
from jax import lax
from jax.experimental import pallas as pl
from jax.experimental.pallas import tpu as pltpu

F32 = jnp.float32
BF16 = jnp.bfloat16

D_MODEL = 2048
DEPTH = 4
CONV_WIDTH = 4096
CONV_KERNEL = 31
CONV_HALO = 16
N_HEADS = 16
QK_NOPE_DIM = 128
QK_ROPE_DIM = 64
HALF_ROPE = QK_ROPE_DIM // 2
QK_DIM = QK_NOPE_DIM + QK_ROPE_DIM
V_HEAD_DIM = 128
Q_LORA_RANK = 512
KV_LORA_RANK = 512
MLA_WIDTH = N_HEADS * V_HEAD_DIM
ROPE_THETA = 10000.0
EPS = 1e-6
LANES = 128
MOD_ROWS = 8
VMEM_LIMIT_BYTES = 56 * 1024 * 1024
Q_PRESCALE = math.log2(math.e) / math.sqrt(QK_DIM)
NT_DIMS = (((1,), (1,)), ((), ()))
ATTN_UNROLL = 4


def _params(*sem):
    return pltpu.CompilerParams(dimension_semantics=sem, vmem_limit_bytes=VMEM_LIMIT_BYTES)


def _silu(v):
    return v * jax.nn.sigmoid(v)


def _ada_body(c_ref, w_ref, b_ref, o_ref):
    c = c_ref[...]
    ca = _silu(c).astype(BF16)
    o_ref[0] = jnp.dot(ca, w_ref[0].astype(BF16), preferred_element_type=F32) + b_ref[0]


def _ada_mod(c_all, ada_w, ada_b):
    n = 3 * D_MODEL
    tn = 1024
    return pl.pallas_call(
        _ada_body,
        grid=(DEPTH, n // tn),
        in_specs=[pl.BlockSpec((MOD_ROWS, D_MODEL), lambda i, j: (0, 0)),
                  pl.BlockSpec((1, D_MODEL, tn), lambda i, j: (i, 0, j)),
                  pl.BlockSpec((1, 1, tn), lambda i, j: (i, 0, j))],
        out_specs=pl.BlockSpec((1, MOD_ROWS, tn), lambda i, j: (i, 0, j)),
        out_shape=jax.ShapeDtypeStruct((DEPTH, MOD_ROWS, n), F32),
        compiler_params=_params("parallel", "parallel"),
        name="ada_mod",
    )(c_all, ada_w, ada_b.reshape(DEPTH, 1, n))


def _modnorm_to(x_ref, g_ref, sc_ref, sh_ref, h_ref, tm, chunk):
    g = g_ref[...]
    sc = 1.0 + sc_ref[...]
    sh = sh_ref[...]

    def body(c, carry):
        r = pl.multiple_of(c * chunk, chunk)
        x = x_ref[pl.ds(r, chunk), :]
        ms = jnp.mean(x * x, axis=-1, keepdims=True)
        y = x * lax.rsqrt(ms + EPS) * g
        h_ref[pl.ds(r, chunk), :] = (y * sc + sh).astype(BF16)
        return carry

    lax.fori_loop(0, tm // chunk, body, 0)


def _conv_in_body(x_ref, g_ref, sc_ref, sh_ref, wa_ref, wg_ref, wz_ref, y_ref, z_ref, h_ref, *, tm):
    @pl.when(pl.program_id(2) == 0)
    def _():
        _modnorm_to(x_ref, g_ref, sc_ref, sh_ref, h_ref, tm, min(tm, 64))

    h = h_ref[...]
    a = jnp.dot(h, wa_ref[...], preferred_element_type=F32)
    gl = jnp.dot(h, wg_ref[...], preferred_element_type=F32)
    y_ref[...] = a * jax.nn.sigmoid(gl)
    z = jnp.dot(h, wz_ref[...], preferred_element_type=F32)
    z_ref[...] = _silu(z)


def _conv_in(x, g, scale, shift, w_in):
    b, l, _ = x.shape
    c = CONV_WIDTH
    tm = min(1024, l)
    tn = 512
    nj = c // tn
    return pl.pallas_call(
        functools.partial(_conv_in_body, tm=tm),
        grid=(b, l // tm, nj),
        in_specs=[pl.BlockSpec((None, tm, D_MODEL), lambda bi, i, j: (bi, i, 0)),
                  pl.BlockSpec((1, D_MODEL), lambda bi, i, j: (0, 0)),
                  pl.BlockSpec((None, 1, D_MODEL), lambda bi, i, j: (bi, 0, 0)),
                  pl.BlockSpec((None, 1, D_MODEL), lambda bi, i, j: (bi, 0, 0)),
                  pl.BlockSpec((D_MODEL, tn), lambda bi, i, j: (0, j)),
                  pl.BlockSpec((D_MODEL, tn), lambda bi, i, j: (0, j + nj)),
                  pl.BlockSpec((D_MODEL, tn), lambda bi, i, j: (0, j + 2 * nj))],
        out_specs=[pl.BlockSpec((None, tm, tn), lambda bi, i, j: (bi, i, j)),
                   pl.BlockSpec((None, tm, tn), lambda bi, i, j: (bi, i, j))],
        out_shape=[jax.ShapeDtypeStruct((b, l, c), F32), jax.ShapeDtypeStruct((b, l, c), F32)],
        scratch_shapes=[pltpu.VMEM((tm, D_MODEL), BF16)],
        compiler_params=_params("parallel", "parallel", "arbitrary"),
        name="conv_in",
    )(x, g, scale, shift, w_in, w_in, w_in)


def _dwconv_body(yp_ref, ym_ref, yn_ref, zs_ref, w_ref, b_ref, lg_ref, lb_ref, o_ref,
                 xbuf, shbuf, cbuf, *, tl, strip, ln_rows):
    i = pl.program_id(1)
    n = pl.num_programs(1)
    halo = CONV_HALO
    xbuf[0:halo, :] = jnp.where(i > 0, yp_ref[...], 0.0)
    xbuf[halo:halo + tl, :] = ym_ref[...]
    xbuf[halo + tl:2 * halo + tl, :] = jnp.where(i < n - 1, yn_ref[...], 0.0)

    first = halo - CONV_KERNEL // 2
    sh_rows = tl + 24

    def cb_body(cb, carry):
        lanes = pl.ds(pl.multiple_of(cb * LANES, LANES), LANES)
        for r in range(1, 8):
            shbuf[r, :, :] = xbuf[pl.ds(r, sh_rows), lanes]
        bias = b_ref[:, lanes]
        for s in range(tl // strip):
            acc = jnp.broadcast_to(bias, (strip, LANES))
            for k in range(CONV_KERNEL):
                a8, r = divmod(first + k, 8)
                start = s * strip + 8 * a8
                if r == 0:
                    win = xbuf[pl.ds(start, strip), lanes]
                else:
                    win = shbuf[r, pl.ds(start, strip), :]
                acc = acc + win * w_ref[k:k + 1, lanes]
            cbuf[pl.ds(s * strip, strip), lanes] = acc
        return carry

    lax.fori_loop(0, CONV_WIDTH // LANES, cb_body, 0)

    lg = lg_ref[...]
    lb = lb_ref[...]

    def ln_body(t, carry):
        rows = pl.ds(pl.multiple_of(t * ln_rows, ln_rows), ln_rows)
        v = cbuf[rows, :]
        mu = jnp.mean(v, axis=-1, keepdims=True)
        vc = v - mu
        var = jnp.mean(vc * vc, axis=-1, keepdims=True)
        yn = vc * lax.rsqrt(var + EPS) * lg + lb
        o_ref[rows, :] = (_silu(yn) * zs_ref[rows, :]).astype(BF16)
        return carry

    lax.fori_loop(0, tl // ln_rows, ln_body, 0)


def _dwconv(y, zs, dw_w, dw_b, ln_g, ln_b):
    b, l, c = y.shape
    tl = min(256, l)
    hb = tl // CONV_HALO
    nhb = l // CONV_HALO
    body = functools.partial(_dwconv_body, tl=tl, strip=min(64, tl), ln_rows=min(32, tl))
    row = lambda bi, i: (0, 0)
    return pl.pallas_call(
        body,
        grid=(b, l // tl),
        in_specs=[pl.BlockSpec((None, CONV_HALO, c), lambda bi, i: (bi, jnp.maximum(i * hb - 1, 0), 0)),
                  pl.BlockSpec((None, tl, c), lambda bi, i: (bi, i, 0)),
                  pl.BlockSpec((None, CONV_HALO, c), lambda bi, i: (bi, jnp.minimum((i + 1) * hb, nhb - 1), 0)),
                  pl.BlockSpec((None, tl, c), lambda bi, i: (bi, i, 0)),
                  pl.BlockSpec((CONV_KERNEL, c), row),
                  pl.BlockSpec((1, c), row),
                  pl.BlockSpec((1, c), row),
                  pl.BlockSpec((1, c), row)],
        out_specs=pl.BlockSpec((None, tl, c), lambda bi, i: (bi, i, 0)),
        out_shape=jax.ShapeDtypeStruct((b, l, c), BF16),
        scratch_shapes=[pltpu.VMEM((tl + 2 * CONV_HALO, c), F32),
                        pltpu.VMEM((8, tl + 24, LANES), F32),
                        pltpu.VMEM((tl, c), F32)],
        compiler_params=_params("parallel", "parallel"),
        name="dwconv_ln",
    )(y, y, y, zs, dw_w, dw_b.reshape(1, c), ln_g.reshape(1, c), ln_b.reshape(1, c))


def _out_proj_body(a_ref, w_ref, x_ref, gt_ref, o_ref):
    y = jnp.dot(a_ref[...], w_ref[...], preferred_element_type=F32)
    o_ref[...] = x_ref[...] + gt_ref[...] * y


def _out_proj_norm_body(a_ref, w_ref, x_ref, gt_ref, g_ref, o_ref):
    y = jnp.dot(a_ref[...], w_ref[...], preferred_element_type=F32)
    xn = x_ref[...] + gt_ref[...] * y
    ms = jnp.mean(xn * xn, axis=-1, keepdims=True)
    o_ref[...] = xn * lax.rsqrt(ms + EPS) * g_ref[...]


def _out_proj(a, w, x, gate, final_g=None):
    b, l, k = a.shape
    if final_g is None:
        tm, tn, body, extra, extra_specs = min(1024, l), 512, _out_proj_body, (), []
    else:
        tm, tn, body, extra = min(512, l), D_MODEL, _out_proj_norm_body, (final_g,)
        extra_specs = [pl.BlockSpec((1, D_MODEL), lambda bi, i, j: (0, 0))]
    return pl.pallas_call(
        body,
        grid=(b, l // tm, D_MODEL // tn),
        in_specs=[pl.BlockSpec((None, tm, k), lambda bi, i, j: (bi, i, 0)),
                  pl.BlockSpec((k, tn), lambda bi, i, j: (0, j)),
                  pl.BlockSpec((None, tm, tn), lambda bi, i, j: (bi, i, j)),
                  pl.BlockSpec((None, 1, tn), lambda bi, i, j: (bi, 0, j))] + extra_specs,
        out_specs=pl.BlockSpec((None, tm, tn), lambda bi, i, j: (bi, i, j)),
        out_shape=jax.ShapeDtypeStruct((b, l, D_MODEL), F32),
        compiler_params=_params("parallel", "parallel", "parallel"),
        name="out_proj",
    )(a, w, x, gate, *extra)


def _mla_in_body(x_ref, g_ref, sc_ref, sh_ref, wq_ref, wkv_ref, wka_ref, wkb_ref, wz_ref,
                 qn_ref, kvn_ref, cc_ref, ss_ref,
                 cq_out, ckv_out, kr_out, zs_out, h_ref, *, tm, zchunk):
    _modnorm_to(x_ref, g_ref, sc_ref, sh_ref, h_ref, tm, min(tm, 64))
    h = h_ref[...]

    def rms(v, gain):
        ms = jnp.mean(v * v, axis=-1, keepdims=True)
        return (v * lax.rsqrt(ms + EPS) * gain).astype(BF16)

    cq_out[...] = rms(jnp.dot(h, wq_ref[...], preferred_element_type=F32), qn_ref[...])
    ckv_out[...] = rms(jnp.dot(h, wkv_ref[...], preferred_element_type=F32), kvn_ref[...])
    ka = jnp.dot(h, wka_ref[...], preferred_element_type=F32)
    kb = jnp.dot(h, wkb_ref[...], preferred_element_type=F32)
    kr = ka * cc_ref[...] + kb * ss_ref[...]
    kr_out[...] = kr[:, :QK_ROPE_DIM].astype(BF16)
    for c in range(MLA_WIDTH // zchunk):
        z = jnp.dot(h, wz_ref[:, c * zchunk:(c + 1) * zchunk], preferred_element_type=F32)
        zs_out[:, c * zchunk:(c + 1) * zchunk] = _silu(z)


def _mla_in(x, g, scale, shift, wq, wkv, wka, wkb, wz, q_norm, kv_norm, cc, ss):
    b, l, _ = x.shape
    tm = min(256, l)
    const = lambda bi, i: (0, 0)
    rows = lambda bi, i: (bi, i, 0)
    per_b = lambda bi, i: (bi, 0, 0)
    return pl.pallas_call(
        functools.partial(_mla_in_body, tm=tm, zchunk=512),
        grid=(b, l // tm),
        in_specs=[pl.BlockSpec((None, tm, D_MODEL), rows),
                  pl.BlockSpec((1, D_MODEL), const),
                  pl.BlockSpec((None, 1, D_MODEL), per_b),
                  pl.BlockSpec((None, 1, D_MODEL), per_b),
                  pl.BlockSpec((D_MODEL, Q_LORA_RANK), const),
                  pl.BlockSpec((D_MODEL, KV_LORA_RANK), const),
                  pl.BlockSpec((D_MODEL, LANES), const),
                  pl.BlockSpec((D_MODEL, LANES), const),
                  pl.BlockSpec((D_MODEL, MLA_WIDTH), const),
                  pl.BlockSpec((1, Q_LORA_RANK), const),
                  pl.BlockSpec((1, KV_LORA_RANK), const),
                  pl.BlockSpec((tm, LANES), lambda bi, i: (i, 0)),
                  pl.BlockSpec((tm, LANES), lambda bi, i: (i, 0))],
        out_specs=[pl.BlockSpec((None, tm, Q_LORA_RANK), rows),
                   pl.BlockSpec((None, tm, KV_LORA_RANK), rows),
                   pl.BlockSpec((None, tm, QK_ROPE_DIM), rows),
                   pl.BlockSpec((None, tm, MLA_WIDTH), rows)],
        out_shape=[jax.ShapeDtypeStruct((b, l, Q_LORA_RANK), BF16),
                   jax.ShapeDtypeStruct((b, l, KV_LORA_RANK), BF16),
                   jax.ShapeDtypeStruct((b, l, QK_ROPE_DIM), BF16),
                   jax.ShapeDtypeStruct((b, l, MLA_WIDTH), F32)],
        scratch_shapes=[pltpu.VMEM((tm, D_MODEL), BF16)],
        compiler_params=_params("parallel", "parallel"),
        name="mla_in",
    )(x, g, scale, shift, wq, wkv, wka, wkb, wz, q_norm, kv_norm, cc, ss)


def _mla_up_body(cq_ref, ckv_ref, kr_ref, cost_ref, sint_ref, wuqt_ref, wuk_ref, wuvt_ref,
                 qt_out, k_out, vt_out):
    cq = cq_ref[...]
    ckv = ckv_ref[...]
    cost = cost_ref[...]
    sint = sint_ref[...]
    qt = lax.dot_general(wuqt_ref[...], cq, NT_DIMS, preferred_element_type=F32)
    for h in range(N_HEADS):
        r0 = h * QK_DIM
        r1 = r0 + QK_NOPE_DIM
        r2 = r1 + HALF_ROPE
        r3 = r2 + HALF_ROPE
        x1 = qt[r1:r2]
        x2 = qt[r2:r3]
        qt_out[r0:r1, :] = (qt[r0:r1] * Q_PRESCALE).astype(BF16)
        qt_out[r1:r2, :] = ((x1 * cost - x2 * sint) * Q_PRESCALE).astype(BF16)
        qt_out[r2:r3, :] = ((x2 * cost + x1 * sint) * Q_PRESCALE).astype(BF16)
    kn = jnp.dot(ckv, wuk_ref[...], preferred_element_type=F32)
    kr = kr_ref[...]
    for h in range(N_HEADS):
        k_out[h, :, 0:QK_NOPE_DIM] = kn[:, h * QK_NOPE_DIM:(h + 1) * QK_NOPE_DIM].astype(BF16)
        k_out[h, :, QK_NOPE_DIM:QK_DIM] = kr
    vt = lax.dot_general(wuvt_ref[...], ckv, NT_DIMS, preferred_element_type=F32)
    vt_out[...] = vt.astype(BF16)


def _mla_up(cq, ckv, kr, cost, sint, wuqt, wuk, wuvt):
    b, l, _ = cq.shape
    tm = min(256, l)
    const = lambda bi, i: (0, 0)
    rows = lambda bi, i: (bi, i, 0)
    cols = lambda bi, i: (bi, 0, i)
    return pl.pallas_call(
        _mla_up_body,
        grid=(b, l // tm),
        in_specs=[pl.BlockSpec((None, tm, Q_LORA_RANK), rows),
                  pl.BlockSpec((None, tm, KV_LORA_RANK), rows),
                  pl.BlockSpec((None, tm, QK_ROPE_DIM), rows),
                  pl.BlockSpec((HALF_ROPE, tm), lambda bi, i: (0, i)),
                  pl.BlockSpec((HALF_ROPE, tm), lambda bi, i: (0, i)),
                  pl.BlockSpec((N_HEADS * QK_DIM, Q_LORA_RANK), const),
                  pl.BlockSpec((KV_LORA_RANK, N_HEADS * QK_NOPE_DIM), const),
                  pl.BlockSpec((MLA_WIDTH, KV_LORA_RANK), const)],
        out_specs=[pl.BlockSpec((None, N_HEADS * QK_DIM, tm), cols),
                   pl.BlockSpec((None, N_HEADS, tm, QK_DIM), lambda bi, i: (bi, 0, i, 0)),
                   pl.BlockSpec((None, MLA_WIDTH, tm), cols)],
        out_shape=[jax.ShapeDtypeStruct((b, N_HEADS * QK_DIM, l), BF16),
                   jax.ShapeDtypeStruct((b, N_HEADS, l, QK_DIM), BF16),
                   jax.ShapeDtypeStruct((b, MLA_WIDTH, l), BF16)],
        compiler_params=_params("parallel", "parallel"),
        name="mla_up",
    )(cq, ckv, kr, cost, sint, wuqt, wuk, wuvt)


def _attn_body(qt_ref, k_ref, vt_ref, zs_ref, o_ref, acc_ref, m_ref, l_ref, *stage_refs, nchunks, tk):
    bufs = tuple(stage_refs[4 * u:4 * u + 4] for u in range(ATTN_UNROLL))

    def scores(c, s_ref, cm_ref):
        r = pl.multiple_of(c * tk, tk)
        s = jnp.dot(k_ref[pl.ds(r, tk), :], qt_ref[...], preferred_element_type=F32)
        s_ref[...] = s
        cm_ref[...] = jnp.max(s, axis=0, keepdims=True)

    def softmax(s_ref, cm_ref, p_ref, al_ref):
        m_prev = m_ref[...]
        m_new = jnp.maximum(m_prev, cm_ref[...])
        alpha = jnp.exp2(m_prev - m_new)
        p = jnp.exp2(s_ref[...] - m_new)
        l_ref[...] = alpha * l_ref[...] + jnp.sum(p, axis=0, keepdims=True)
        p_ref[...] = p.astype(BF16)
        al_ref[...] = alpha
        m_ref[...] = m_new

    def values(c, p_ref, al_ref):
        r = pl.multiple_of(c * tk, tk)
        pv = jnp.dot(vt_ref[:, pl.ds(r, tk)], p_ref[...], preferred_element_type=F32)
        acc_ref[...] = al_ref[...] * acc_ref[...] + pv

    m_ref[...] = jnp.full(m_ref.shape, -jnp.inf, F32)
    l_ref[...] = jnp.zeros(l_ref.shape, F32)
    acc_ref[...] = jnp.zeros(acc_ref.shape, F32)
    last = ATTN_UNROLL - 1
    bufs[last][2][...] = jnp.zeros(bufs[last][2].shape, BF16)
    bufs[last][3][...] = jnp.ones(bufs[last][3].shape, F32)
    scores(0, bufs[0][0], bufs[0][1])

    def body(j, carry):
        c0 = ATTN_UNROLL * j
        for u in range(ATTN_UNROLL):
            s_cur, cm_cur, p_cur, al_cur = bufs[u]
            s_nxt, cm_nxt, _, _ = bufs[(u + 1) % ATTN_UNROLL]
            _, _, p_prv, al_prv = bufs[(u - 1) % ATTN_UNROLL]
            softmax(s_cur, cm_cur, p_cur, al_cur)
            values(jnp.maximum(c0 + u - 1, 0), p_prv, al_prv)
            scores(jnp.minimum(c0 + u + 1, nchunks - 1), s_nxt, cm_nxt)
        return carry

    lax.fori_loop(0, nchunks // ATTN_UNROLL, body, 0)
    values(nchunks - 1, bufs[last][2], bufs[last][3])
    o = acc_ref[...] / l_ref[...]
    o_ref[...] = (o.T * zs_ref[...]).astype(BF16)


def _attention(qt, k, vt, zs):
    b, _, l = qt.shape
    tq = min(512, l)
    tk = min(1024, l // ATTN_UNROLL)
    nchunks = l // tk
    assert nchunks % ATTN_UNROLL == 0
    stat = pltpu.VMEM((1, tq), F32)
    return pl.pallas_call(
        functools.partial(_attn_body, nchunks=nchunks, tk=tk),
        grid=(b, N_HEADS, l // tq),
        in_specs=[pl.BlockSpec((None, QK_DIM, tq), lambda bi, h, qi: (bi, h, qi)),
                  pl.BlockSpec((None, None, l, QK_DIM), lambda bi, h, qi: (bi, h, 0, 0)),
                  pl.BlockSpec((None, V_HEAD_DIM, l), lambda bi, h, qi: (bi, h, 0)),
                  pl.BlockSpec((None, tq, V_HEAD_DIM), lambda bi, h, qi: (bi, qi, h))],
        out_specs=pl.BlockSpec((None, tq, V_HEAD_DIM), lambda bi, h, qi: (bi, qi, h)),
        out_shape=jax.ShapeDtypeStruct((b, l, MLA_WIDTH), BF16),
        scratch_shapes=[pltpu.VMEM((V_HEAD_DIM, tq), F32), stat, stat]
        + [pltpu.VMEM((tk, tq), F32), stat, pltpu.VMEM((tk, tq), BF16), stat] * ATTN_UNROLL,
        compiler_params=_params("parallel", "parallel", "parallel"),
        name="mla_attn",
    )(qt, k, vt, zs)


def _rope_tables(length):
    inv = 1.0 / (ROPE_THETA ** (jnp.arange(0, QK_ROPE_DIM, 2, dtype=F32) / QK_ROPE_DIM))
    ang = jnp.arange(length, dtype=F32)[:, None] * inv[None, :]
    return jnp.cos(ang), jnp.sin(ang)


def _prep_weights(conv_w_in, conv_w_out, mla_w_in, mla_w_uq, mla_w_ukv, mla_w_out):
    q0, q1, q2 = Q_LORA_RANK, Q_LORA_RANK + KV_LORA_RANK, Q_LORA_RANK + KV_LORA_RANK + QK_ROPE_DIM
    pad = ((0, 0), (0, 0), (0, LANES - QK_ROPE_DIM))
    wkr = mla_w_in[:, :, q1:q2]
    wkr_swapped = jnp.concatenate([wkr[..., HALF_ROPE:], wkr[..., :HALF_ROPE]], axis=-1)
    n_mla = mla_w_ukv.shape[0]
    wukv = mla_w_ukv.reshape(n_mla, KV_LORA_RANK, N_HEADS, QK_NOPE_DIM + V_HEAD_DIM)
    return dict(
        conv_w_in=conv_w_in.astype(BF16),
        conv_w_out=conv_w_out.astype(BF16),
        wq=mla_w_in[:, :, :q0].astype(BF16),
        wkv=mla_w_in[:, :, q0:q1].astype(BF16),
        wka=jnp.pad(wkr, pad).astype(BF16),
        wkb=jnp.pad(wkr_swapped, pad).astype(BF16),
        wz=mla_w_in[:, :, q2:].astype(BF16),
        wuqt=jnp.swapaxes(mla_w_uq, 1, 2).astype(BF16),
        wuk=wukv[..., :QK_NOPE_DIM].reshape(n_mla, KV_LORA_RANK, N_HEADS * QK_NOPE_DIM).astype(BF16),
        wuvt=jnp.swapaxes(wukv[..., QK_NOPE_DIM:].reshape(n_mla, KV_LORA_RANK, MLA_WIDTH), 1, 2).astype(BF16),
        mla_w_out=mla_w_out.astype(BF16),
    )


def _trunk(x, mod, norm_g, wts, conv_dw_w, conv_dw_b, conv_ln_g, conv_ln_b,
           mla_q_norm, mla_kv_norm, final_g):
    assert DEPTH % 2 == 0
    length = x.shape[1]
    cos, sin = _rope_tables(length)
    lane_pad = ((0, 0), (0, LANES - QK_ROPE_DIM))
    cc = jnp.pad(jnp.concatenate([cos, cos], axis=-1), lane_pad)
    ss = jnp.pad(jnp.concatenate([-sin, sin], axis=-1), lane_pad)
    cost, sint = cos.T, sin.T
    for i in range(DEPTH):
        shift = mod[i, :, 0][:, None, :]
        scale = mod[i, :, 1][:, None, :]
        gate = mod[i, :, 2][:, None, :]
        g = norm_g[i][None, :]
        j = i // 2
        if i % 2 == 0:
            y, zs = _conv_in(x, g, scale, shift, wts["conv_w_in"][j])
            yc = _dwconv(y, zs, conv_dw_w[j], conv_dw_b[j], conv_ln_g[j], conv_ln_b[j])
            x = _out_proj(yc, wts["conv_w_out"][j], x, gate)
        else:
            cq, ckv, kr, zs = _mla_in(x, g, scale, shift, wts["wq"][j], wts["wkv"][j], wts["wka"][j],
                                      wts["wkb"][j], wts["wz"][j], mla_q_norm[j][None, :],
                                      mla_kv_norm[j][None, :], cc, ss)
            qt, k, vt = _mla_up(cq, ckv, kr, cost, sint, wts["wuqt"][j], wts["wuk"][j], wts["wuvt"][j])
            og = _attention(qt, k, vt, zs)
            x = _out_proj(og, wts["mla_w_out"][j], x, gate, final_g[None, :] if i == DEPTH - 1 else None)
    return x


def kernel(x_prompt, x_sample, c_prompt, c_sample, norm_g, ada_w, ada_b, conv_w_in, conv_dw_w, conv_dw_b,
           conv_ln_g, conv_ln_b, conv_w_out, mla_w_in, mla_q_norm, mla_kv_norm, mla_w_uq, mla_w_ukv,
           mla_w_out, final_g):
    nb_p, nb_s = c_prompt.shape[0], c_sample.shape[0]
    c_all = jnp.concatenate([c_prompt, c_sample, jnp.zeros((MOD_ROWS - nb_p - nb_s, D_MODEL), F32)], axis=0)
    mod = _ada_mod(c_all, ada_w, ada_b).reshape(DEPTH, MOD_ROWS, 3, D_MODEL)
    wts = _prep_weights(conv_w_in, conv_w_out, mla_w_in, mla_w_uq, mla_w_ukv, mla_w_out)
    rest = (norm_g, wts, conv_dw_w, conv_dw_b, conv_ln_g, conv_ln_b, mla_q_norm, mla_kv_norm, final_g)
    y_prompt = _trunk(x_prompt, mod[:, :nb_p], *rest)
    y_sample = _trunk(x_sample, mod[:, nb_p:nb_p + nb_s], *rest)
    return (y_prompt, y_sample)
```

```python
import functools
import math

import jax
import jax.numpy as jnp
from jax import lax
from jax.experimental import pallas as pl
from jax.experimental.pallas import tpu as pltpu

F32 = jnp.float32
BF16 = jnp.bfloat16

D_MODEL = 2048
DEPTH = 4
CONV_WIDTH = 4096
CONV_KERNEL = 31
CONV_HALO = 16
N_HEADS = 16
QK_NOPE_DIM = 128
QK_ROPE_DIM = 64
HALF_ROPE = QK_ROPE_DIM // 2
QK_DIM = QK_NOPE_DIM + QK_ROPE_DIM
V_HEAD_DIM = 128
Q_LORA_RANK = 512
KV_LORA_RANK = 512
MLA_WIDTH = N_HEADS * V_HEAD_DIM
ROPE_THETA = 10000.0
EPS = 1e-6
LANES = 128
MOD_ROWS = 8
VMEM_LIMIT_BYTES = 56 * 1024 * 1024
Q_PRESCALE = math.log2(math.e) / math.sqrt(QK_DIM)
NT_DIMS = (((1,), (1,)), ((), ()))
ATTN_BUFS = 3
ATTN_KEY_CHUNK = 1024


def _params(*sem):
    return pltpu.CompilerParams(dimension_semantics=sem, vmem_limit_bytes=VMEM_LIMIT_BYTES)


def _silu(v):
    return v * jax.nn.sigmoid(v)


def _ada_body(c_ref, w_ref, b_ref, o_ref):
    c = c_ref[...]
    ca = _silu(c).astype(BF16)
    o_ref[0] = jnp.dot(ca, w_ref[0].astype(BF16), preferred_element_type=F32) + b_ref[0]


def _ada_mod(c_all, ada_w, ada_b):
    n = 3 * D_MODEL
    tn = 1024
    return pl.pallas_call(
        _ada_body,
        grid=(DEPTH, n // tn),
        in_specs=[pl.BlockSpec((MOD_ROWS, D_MODEL), lambda i, j: (0, 0)),
                  pl.BlockSpec((1, D_MODEL, tn), lambda i, j: (i, 0, j)),
                  pl.BlockSpec((1, 1, tn), lambda i, j: (i, 0, j))],
        out_specs=pl.BlockSpec((1, MOD_ROWS, tn), lambda i, j: (i, 0, j)),
        out_shape=jax.ShapeDtypeStruct((DEPTH, MOD_ROWS, n), F32),
        compiler_params=_params("parallel", "parallel"),
        name="ada_mod",
    )(c_all, ada_w, ada_b.reshape(DEPTH, 1, n))


def _modnorm_to(x_ref, g_ref, sc_ref, sh_ref, h_ref, tm, chunk):
    g = g_ref[...]
    sc = 1.0 + sc_ref[...]
    sh = sh_ref[...]

    def body(c, carry):
        r = pl.multiple_of(c * chunk, chunk)
        x = x_ref[pl.ds(r, chunk), :]
        ms = jnp.mean(x * x, axis=-1, keepdims=True)
        y = x * lax.rsqrt(ms + EPS) * g
        h_ref[pl.ds(r, chunk), :] = (y * sc + sh).astype(BF16)
        return carry

    lax.fori_loop(0, tm // chunk, body, 0)


def _conv_in_body(x_ref, g_ref, sc_ref, sh_ref, wa_ref, wg_ref, wz_ref, y_ref, z_ref, h_ref, *, tm):
    @pl.when(pl.program_id(2) == 0)
    def _():
        _modnorm_to(x_ref, g_ref, sc_ref, sh_ref, h_ref, tm, min(tm, 64))

    h = h_ref[...]
    a = jnp.dot(h, wa_ref[...], preferred_element_type=F32)
    gl = jnp.dot(h, wg_ref[...], preferred_element_type=F32)
    y_ref[...] = a * jax.nn.sigmoid(gl)
    z = jnp.dot(h, wz_ref[...], preferred_element_type=F32)
    z_ref[...] = _silu(z)


def _conv_in(x, g, scale, shift, w_in):
    b, l, _ = x.shape
    c = CONV_WIDTH
    tm = min(1024, l)
    tn = 512
    nj = c // tn
    return pl.pallas_call(
        functools.partial(_conv_in_body, tm=tm),
        grid=(b, l // tm, nj),
        in_specs=[pl.BlockSpec((None, tm, D_MODEL), lambda bi, i, j: (bi, i, 0)),
                  pl.BlockSpec((1, D_MODEL), lambda bi, i, j: (0, 0)),
                  pl.BlockSpec((None, 1, D_MODEL), lambda bi, i, j: (bi, 0, 0)),
                  pl.BlockSpec((None, 1, D_MODEL), lambda bi, i, j: (bi, 0, 0)),
                  pl.BlockSpec((D_MODEL, tn), lambda bi, i, j: (0, j)),
                  pl.BlockSpec((D_MODEL, tn), lambda bi, i, j: (0, j + nj)),
                  pl.BlockSpec((D_MODEL, tn), lambda bi, i, j: (0, j + 2 * nj))],
        out_specs=[pl.BlockSpec((None, tm, tn), lambda bi, i, j: (bi, i, j)),
                   pl.BlockSpec((None, tm, tn), lambda bi, i, j: (bi, i, j))],
        out_shape=[jax.ShapeDtypeStruct((b, l, c), F32), jax.ShapeDtypeStruct((b, l, c), F32)],
        scratch_shapes=[pltpu.VMEM((tm, D_MODEL), BF16)],
        compiler_params=_params("parallel", "parallel", "arbitrary"),
        name="conv_in",
    )(x, g, scale, shift, w_in, w_in, w_in)


def _dwconv_body(yp_ref, ym_ref, yn_ref, zs_ref, w_ref, b_ref, lg_ref, lb_ref, o_ref,
                 xbuf, shbuf, cbuf, *, tl, strip, ln_rows):
    i = pl.program_id(1)
    n = pl.num_programs(1)
    halo = CONV_HALO
    xbuf[0:halo, :] = jnp.where(i > 0, yp_ref[...], 0.0)
    xbuf[halo:halo + tl, :] = ym_ref[...]
    xbuf[halo + tl:2 * halo + tl, :] = jnp.where(i < n - 1, yn_ref[...], 0.0)

    first = halo - CONV_KERNEL // 2
    sh_rows = tl + 24

    def cb_body(cb, carry):
        lanes = pl.ds(pl.multiple_of(cb * LANES, LANES), LANES)
        for r in range(1, 8):
            shbuf[r, :, :] = xbuf[pl.ds(r, sh_rows), lanes]
        bias = b_ref[:, lanes]
        for s in range(tl // strip):
            acc = jnp.broadcast_to(bias, (strip, LANES))
            for k in range(CONV_KERNEL):
                a8, r = divmod(first + k, 8)
                start = s * strip + 8 * a8
                if r == 0:
                    win = xbuf[pl.ds(start, strip), lanes]
                else:
                    win = shbuf[r, pl.ds(start, strip), :]
                acc = acc + win * w_ref[k:k + 1, lanes]
            cbuf[pl.ds(s * strip, strip), lanes] = acc
        return carry

    lax.fori_loop(0, CONV_WIDTH // LANES, cb_body, 0)

    lg = lg_ref[...]
    lb = lb_ref[...]

    def ln_body(t, carry):
        rows = pl.ds(pl.multiple_of(t * ln_rows, ln_rows), ln_rows)
        v = cbuf[rows, :]
        mu = jnp.mean(v, axis=-1, keepdims=True)
        vc = v - mu
        var = jnp.mean(vc * vc, axis=-1, keepdims=True)
        yn = vc * lax.rsqrt(var + EPS) * lg + lb
        o_ref[rows, :] = (_silu(yn) * zs_ref[rows, :]).astype(BF16)
        return carry

    lax.fori_loop(0, tl // ln_rows, ln_body, 0)


def _dwconv(y, zs, dw_w, dw_b, ln_g, ln_b):
    b, l, c = y.shape
    tl = min(256, l)
    hb = tl // CONV_HALO
    nhb = l // CONV_HALO
    body = functools.partial(_dwconv_body, tl=tl, strip=min(64, tl), ln_rows=min(32, tl))
    row = lambda bi, i: (0, 0)
    return pl.pallas_call(
        body,
        grid=(b, l // tl),
        in_specs=[pl.BlockSpec((None, CONV_HALO, c), lambda bi, i: (bi, jnp.maximum(i * hb - 1, 0), 0)),
                  pl.BlockSpec((None, tl, c), lambda bi, i: (bi, i, 0)),
                  pl.BlockSpec((None, CONV_HALO, c), lambda bi, i: (bi, jnp.minimum((i + 1) * hb, nhb - 1), 0)),
                  pl.BlockSpec((None, tl, c), lambda bi, i: (bi, i, 0)),
                  pl.BlockSpec((CONV_KERNEL, c), row),
                  pl.BlockSpec((1, c), row),
                  pl.BlockSpec((1, c), row),
                  pl.BlockSpec((1, c), row)],
        out_specs=pl.BlockSpec((None, tl, c), lambda bi, i: (bi, i, 0)),
        out_shape=jax.ShapeDtypeStruct((b, l, c), BF16),
        scratch_shapes=[pltpu.VMEM((tl + 2 * CONV_HALO, c), F32),
                        pltpu.VMEM((8, tl + 24, LANES), F32),
                        pltpu.VMEM((tl, c), F32)],
        compiler_params=_params("parallel", "parallel"),
        name="dwconv_ln",
    )(y, y, y, zs, dw_w, dw_b.reshape(1, c), ln_g.reshape(1, c), ln_b.reshape(1, c))


def _out_proj_body(a_ref, w_ref, x_ref, gt_ref, o_ref):
    y = jnp.dot(a_ref[...], w_ref[...], preferred_element_type=F32)
    o_ref[...] = x_ref[...] + gt_ref[...] * y


def _out_proj_norm_body(a_ref, w_ref, x_ref, gt_ref, g_ref, o_ref):
    y = jnp.dot(a_ref[...], w_ref[...], preferred_element_type=F32)
    xn = x_ref[...] + gt_ref[...] * y
    ms = jnp.mean(xn * xn, axis=-1, keepdims=True)
    o_ref[...] = xn * lax.rsqrt(ms + EPS) * g_ref[...]


def _out_proj(a, w, x, gate, final_g=None):
    b, l, k = a.shape
    if final_g is None:
        tm, tn, body, extra, extra_specs = min(1024, l), 512, _out_proj_body, (), []
    else:
        tm, tn, body, extra = min(512, l), D_MODEL, _out_proj_norm_body, (final_g,)
        extra_specs = [pl.BlockSpec((1, D_MODEL), lambda bi, i, j: (0, 0))]
    return pl.pallas_call(
        body,
        grid=(b, l // tm, D_MODEL // tn),
        in_specs=[pl.BlockSpec((None, tm, k), lambda bi, i, j: (bi, i, 0)),
                  pl.BlockSpec((k, tn), lambda bi, i, j: (0, j)),
                  pl.BlockSpec((None, tm, tn), lambda bi, i, j: (bi, i, j)),
                  pl.BlockSpec((None, 1, tn), lambda bi, i, j: (bi, 0, j))] + extra_specs,
        out_specs=pl.BlockSpec((None, tm, tn), lambda bi, i, j: (bi, i, j)),
        out_shape=jax.ShapeDtypeStruct((b, l, D_MODEL), F32),
        compiler_params=_params("parallel", "parallel", "parallel"),
        name="out_proj",
    )(a, w, x, gate, *extra)


def _mla_in_body(x_ref, g_ref, sc_ref, sh_ref, wq_ref, wkv_ref, wka_ref, wkb_ref, wz_ref,
                 qn_ref, kvn_ref, cc_ref, ss_ref,
                 cq_out, ckv_out, kr_out, zs_out, h_ref, *, tm, zchunk):
    _modnorm_to(x_ref, g_ref, sc_ref, sh_ref, h_ref, tm, min(tm, 64))
    h = h_ref[...]

    def rms(v, gain):
        ms = jnp.mean(v * v, axis=-1, keepdims=True)
        return (v * lax.rsqrt(ms + EPS) * gain).astype(BF16)

    cq_out[...] = rms(jnp.dot(h, wq_ref[...], preferred_element_type=F32), qn_ref[...])
    ckv_out[...] = rms(jnp.dot(h, wkv_ref[...], preferred_element_type=F32), kvn_ref[...])
    ka = jnp.dot(h, wka_ref[...], preferred_element_type=F32)
    kb = jnp.dot(h, wkb_ref[...], preferred_element_type=F32)
    kr = ka * cc_ref[...] + kb * ss_ref[...]
    kr_out[...] = kr[:, :QK_ROPE_DIM].astype(BF16)
    for c in range(MLA_WIDTH // zchunk):
        z = jnp.dot(h, wz_ref[:, c * zchunk:(c + 1) * zchunk], preferred_element_type=F32)
        zs_out[:, c * zchunk:(c + 1) * zchunk] = _silu(z)


def _mla_in(x, g, scale, shift, wq, wkv, wka, wkb, wz, q_norm, kv_norm, cc, ss):
    b, l, _ = x.shape
    tm = min(256, l)
    const = lambda bi, i: (0, 0)
    rows = lambda bi, i: (bi, i, 0)
    per_b = lambda bi, i: (bi, 0, 0)
    return pl.pallas_call(
        functools.partial(_mla_in_body, tm=tm, zchunk=512),
        grid=(b, l // tm),
        in_specs=[pl.BlockSpec((None, tm, D_MODEL), rows),
                  pl.BlockSpec((1, D_MODEL), const),
                  pl.BlockSpec((None, 1, D_MODEL), per_b),
                  pl.BlockSpec((None, 1, D_MODEL), per_b),
                  pl.BlockSpec((D_MODEL, Q_LORA_RANK), const),
                  pl.BlockSpec((D_MODEL, KV_LORA_RANK), const),
                  pl.BlockSpec((D_MODEL, LANES), const),
                  pl.BlockSpec((D_MODEL, LANES), const),
                  pl.BlockSpec((D_MODEL, MLA_WIDTH), const),
                  pl.BlockSpec((1, Q_LORA_RANK), const),
                  pl.BlockSpec((1, KV_LORA_RANK), const),
                  pl.BlockSpec((tm, LANES), lambda bi, i: (i, 0)),
                  pl.BlockSpec((tm, LANES), lambda bi, i: (i, 0))],
        out_specs=[pl.BlockSpec((None, tm, Q_LORA_RANK), rows),
                   pl.BlockSpec((None, tm, KV_LORA_RANK), rows),
                   pl.BlockSpec((None, tm, QK_ROPE_DIM), rows),
                   pl.BlockSpec((None, tm, MLA_WIDTH), rows)],
        out_shape=[jax.ShapeDtypeStruct((b, l, Q_LORA_RANK), BF16),
                   jax.ShapeDtypeStruct((b, l, KV_LORA_RANK), BF16),
                   jax.ShapeDtypeStruct((b, l, QK_ROPE_DIM), BF16),
                   jax.ShapeDtypeStruct((b, l, MLA_WIDTH), F32)],
        scratch_shapes=[pltpu.VMEM((tm, D_MODEL), BF16)],
        compiler_params=_params("parallel", "parallel"),
        name="mla_in",
    )(x, g, scale, shift, wq, wkv, wka, wkb, wz, q_norm, kv_norm, cc, ss)


def _mla_up_body(cq_ref, ckv_ref, kr_ref, cost_ref, sint_ref, wuqt_ref, wuk_ref, wuvt_ref,
                 qt_out, k_out, vt_out):
    cq = cq_ref[...]
    ckv = ckv_ref[...]
    cost = cost_ref[...]
    sint = sint_ref[...]
    qt = lax.dot_general(wuqt_ref[...], cq, NT_DIMS, preferred_element_type=F32)
    for h in range(N_HEADS):
        r0 = h * QK_DIM
        r1 = r0 + QK_NOPE_DIM
        r2 = r1 + HALF_ROPE
        r3 = r2 + HALF_ROPE
        x1 = qt[r1:r2]
        x2 = qt[r2:r3]
        qt_out[r0:r1, :] = (qt[r0:r1] * Q_PRESCALE).astype(BF16)
        qt_out[r1:r2, :] = ((x1 * cost - x2 * sint) * Q_PRESCALE).astype(BF16)
        qt_out[r2:r3, :] = ((x2 * cost + x1 * sint) * Q_PRESCALE).astype(BF16)
    kn = jnp.dot(ckv, wuk_ref[...], preferred_element_type=F32)
    kr = kr_ref[...]
    for h in range(N_HEADS):
        k_out[h, :, 0:QK_NOPE_DIM] = kn[:, h * QK_NOPE_DIM:(h + 1) * QK_NOPE_DIM].astype(BF16)
        k_out[h, :, QK_NOPE_DIM:QK_DIM] = kr
    vt = lax.dot_general(wuvt_ref[...], ckv, NT_DIMS, preferred_element_type=F32)
    vt_out[...] = vt.astype(BF16)


def _mla_up(cq, ckv, kr, cost, sint, wuqt, wuk, wuvt):
    b, l, _ = cq.shape
    tm = min(256, l)
    const = lambda bi, i: (0, 0)
    rows = lambda bi, i: (bi, i, 0)
    cols = lambda bi, i: (bi, 0, i)
    return pl.pallas_call(
        _mla_up_body,
        grid=(b, l // tm),
        in_specs=[pl.BlockSpec((None, tm, Q_LORA_RANK), rows),
                  pl.BlockSpec((None, tm, KV_LORA_RANK), rows),
                  pl.BlockSpec((None, tm, QK_ROPE_DIM), rows),
                  pl.BlockSpec((HALF_ROPE, tm), lambda bi, i: (0, i)),
                  pl.BlockSpec((HALF_ROPE, tm), lambda bi, i: (0, i)),
                  pl.BlockSpec((N_HEADS * QK_DIM, Q_LORA_RANK), const),
                  pl.BlockSpec((KV_LORA_RANK, N_HEADS * QK_NOPE_DIM), const),
                  pl.BlockSpec((MLA_WIDTH, KV_LORA_RANK), const)],
        out_specs=[pl.BlockSpec((None, N_HEADS * QK_DIM, tm), cols),
                   pl.BlockSpec((None, N_HEADS, tm, QK_DIM), lambda bi, i: (bi, 0, i, 0)),
                   pl.BlockSpec((None, MLA_WIDTH, tm), cols)],
        out_shape=[jax.ShapeDtypeStruct((b, N_HEADS * QK_DIM, l), BF16),
                   jax.ShapeDtypeStruct((b, N_HEADS, l, QK_DIM), BF16),
                   jax.ShapeDtypeStruct((b, MLA_WIDTH, l), BF16)],
        compiler_params=_params("parallel", "parallel"),
        name="mla_up",
    )(cq, ckv, kr, cost, sint, wuqt, wuk, wuvt)


def _attn_body(qt_ref, k_ref, vt_ref, zs_ref, o_ref, acc_ref, m_ref, l_ref, *stage_refs, nchunks, tk):
    bufs = tuple(stage_refs[4 * u:4 * u + 4] for u in range(ATTN_BUFS))

    def scores(c, s_ref, cm_ref):
        r = c * tk if isinstance(c, int) else pl.multiple_of(c * tk, tk)
        s = jnp.dot(k_ref[pl.ds(r, tk), :], qt_ref[...], preferred_element_type=F32)
        s_ref[...] = s
        cm_ref[...] = jnp.max(s, axis=0, keepdims=True)

    def softmax(s_ref, cm_ref, p_ref, al_ref):
        m_prev = m_ref[...]
        m_new = jnp.maximum(m_prev, cm_ref[...])
        alpha = jnp.exp2(m_prev - m_new)
        p = jnp.exp2(s_ref[...] - m_new)
        l_ref[...] = alpha * l_ref[...] + jnp.sum(p, axis=0, keepdims=True)
        p_ref[...] = p.astype(BF16)
        al_ref[...] = alpha
        m_ref[...] = m_new

    def values(c, p_ref, al_ref):
        r = c * tk if isinstance(c, int) else pl.multiple_of(c * tk, tk)
        pv = jnp.dot(vt_ref[:, pl.ds(r, tk)], p_ref[...], preferred_element_type=F32)
        acc_ref[...] = al_ref[...] * acc_ref[...] + pv

    m_ref[...] = jnp.full(m_ref.shape, -jnp.inf, F32)
    l_ref[...] = jnp.zeros(l_ref.shape, F32)
    acc_ref[...] = jnp.zeros(acc_ref.shape, F32)
    scores(0, bufs[0][0], bufs[0][1])
    for c in range(nchunks):
        s_cur, cm_cur, p_cur, al_cur = bufs[c % ATTN_BUFS]
        softmax(s_cur, cm_cur, p_cur, al_cur)
        if c > 0:
            values(c - 1, *bufs[(c - 1) % ATTN_BUFS][2:])
        if c + 1 < nchunks:
            scores(c + 1, *bufs[(c + 1) % ATTN_BUFS][:2])
    values(nchunks - 1, *bufs[(nchunks - 1) % ATTN_BUFS][2:])
    o = acc_ref[...] / l_ref[...]
    o_ref[...] = (o.T * zs_ref[...]).astype(BF16)


def _attention(qt, k, vt, zs):
    b, _, l = qt.shape
    tq = min(512, l)
    tk = min(ATTN_KEY_CHUNK, l)
    nchunks = l // tk
    stat = pltpu.VMEM((1, tq), F32)
    return pl.pallas_call(
        functools.partial(_attn_body, nchunks=nchunks, tk=tk),
        grid=(b, N_HEADS, l // tq),
        in_specs=[pl.BlockSpec((None, QK_DIM, tq), lambda bi, h, qi: (bi, h, qi)),
                  pl.BlockSpec((None, None, l, QK_DIM), lambda bi, h, qi: (bi, h, 0, 0)),
                  pl.BlockSpec((None, V_HEAD_DIM, l), lambda bi, h, qi: (bi, h, 0)),
                  pl.BlockSpec((None, tq, V_HEAD_DIM), lambda bi, h, qi: (bi, qi, h))],
        out_specs=pl.BlockSpec((None, tq, V_HEAD_DIM), lambda bi, h, qi: (bi, qi, h)),
        out_shape=jax.ShapeDtypeStruct((b, l, MLA_WIDTH), BF16),
        scratch_shapes=[pltpu.VMEM((V_HEAD_DIM, tq), F32), stat, stat]
        + [pltpu.VMEM((tk, tq), F32), stat, pltpu.VMEM((tk, tq), BF16), stat] * ATTN_BUFS,
        compiler_params=_params("parallel", "parallel", "parallel"),
        name="mla_attn",
    )(qt, k, vt, zs)


def _rope_tables(length):
    inv = 1.0 / (ROPE_THETA ** (jnp.arange(0, QK_ROPE_DIM, 2, dtype=F32) / QK_ROPE_DIM))
    ang = jnp.arange(length, dtype=F32)[:, None] * inv[None, :]
    return jnp.cos(ang), jnp.sin(ang)


def _prep_weights(conv_w_in, conv_w_out, mla_w_in, mla_w_uq, mla_w_ukv, mla_w_out):
    q0, q1, q2 = Q_LORA_RANK, Q_LORA_RANK + KV_LORA_RANK, Q_LORA_RANK + KV_LORA_RANK + QK_ROPE_DIM
    pad = ((0, 0), (0, 0), (0, LANES - QK_ROPE_DIM))
    wkr = mla_w_in[:, :, q1:q2]
    wkr_swapped = jnp.concatenate([wkr[..., HALF_ROPE:], wkr[..., :HALF_ROPE]], axis=-1)
    n_mla = mla_w_ukv.shape[0]
    wukv = mla_w_ukv.reshape(n_mla, KV_LORA_RANK, N_HEADS, QK_NOPE_DIM + V_HEAD_DIM)
    return dict(
        conv_w_in=conv_w_in.astype(BF16),
        conv_w_out=conv_w_out.astype(BF16),
        wq=mla_w_in[:, :, :q0].astype(BF16),
        wkv=mla_w_in[:, :, q0:q1].astype(BF16),
        wka=jnp.pad(wkr, pad).astype(BF16),
        wkb=jnp.pad(wkr_swapped, pad).astype(BF16),
        wz=mla_w_in[:, :, q2:].astype(BF16),
        wuqt=jnp.swapaxes(mla_w_uq, 1, 2).astype(BF16),
        wuk=wukv[..., :QK_NOPE_DIM].reshape(n_mla, KV_LORA_RANK, N_HEADS * QK_NOPE_DIM).astype(BF16),
        wuvt=jnp.swapaxes(wukv[..., QK_NOPE_DIM:].reshape(n_mla, KV_LORA_RANK, MLA_WIDTH), 1, 2).astype(BF16),
        mla_w_out=mla_w_out.astype(BF16),
    )


def _trunk(x, mod, norm_g, wts, conv_dw_w, conv_dw_b, conv_ln_g, conv_ln_b,
           mla_q_norm, mla_kv_norm, final_g):
    assert DEPTH % 2 == 0
    length = x.shape[1]
    cos, sin = _rope_tables(length)
    lane_pad = ((0, 0), (0, LANES - QK_ROPE_DIM))
    cc = jnp.pad(jnp.concatenate([cos, cos], axis=-1), lane_pad)
    ss = jnp.pad(jnp.concatenate([-sin, sin], axis=-1), lane_pad)
    cost, sint = cos.T, sin.T
    for i in range(DEPTH):
        shift = mod[i, :, 0][:, None, :]
        scale = mod[i, :, 1][:, None, :]
        gate = mod[i, :, 2][:, None, :]
        g = norm_g[i][None, :]
        j = i // 2
        if i % 2 == 0:
            y, zs = _conv_in(x, g, scale, shift, wts["conv_w_in"][j])
            yc = _dwconv(y, zs, conv_dw_w[j], conv_dw_b[j], conv_ln_g[j], conv_ln_b[j])
            x = _out_proj(yc, wts["conv_w_out"][j], x, gate)
        else:
            cq, ckv, kr, zs = _mla_in(x, g, scale, shift, wts["wq"][j], wts["wkv"][j], wts["wka"][j],
                                      wts["wkb"][j], wts["wz"][j], mla_q_norm[j][None, :],
                                      mla_kv_norm[j][None, :], cc, ss)
            qt, k, vt = _mla_up(cq, ckv, kr, cost, sint, wts["wuqt"][j], wts["wuk"][j], wts["wuvt"][j])
            og = _attention(qt, k, vt, zs)
            x = _out_proj(og, wts["mla_w_out"][j], x, gate, final_g[None, :] if i == DEPTH - 1 else None)
    return x


def kernel(x_prompt, x_sample, c_prompt, c_sample, norm_g, ada_w, ada_b, conv_w_in, conv_dw_w, conv_dw_b,
           conv_ln_g, conv_ln_b, conv_w_out, mla_w_in, mla_q_norm, mla_kv_norm, mla_w_uq, mla_w_ukv,
           mla_w_out, final_g):
    nb_p, nb_s = c_prompt.shape[0], c_sample.shape[0]
    c_all = jnp.concatenate([c_prompt, c_sample, jnp.zeros((MOD_ROWS - nb_p - nb_s, D_MODEL), F32)], axis=0)
    mod = _ada_mod(c_all, ada_w, ada_b).reshape(DEPTH, MOD_ROWS, 3, D_MODEL)
    wts = _prep_weights(conv_w_in, conv_w_out, mla_w_in, mla_w_uq, mla_w_ukv, mla_w_out)
    rest = (norm_g, wts, conv_dw_w, conv_dw_b, conv_ln_g, conv_ln_b, mla_q_norm, mla_kv_norm, final_g)
    y_prompt = _trunk(x_prompt, mod[:, :nb_p], *rest)
    y_sample = _trunk(x_sample, mod[:, nb_p:nb_p + nb_s], *rest)
    return (y_prompt, y_sample)
```

```python
import functools
import math

import jax
import jax.numpy as jnp
from jax import lax
from jax.experimental import pallas as pl
from jax.experimental.pallas import tpu as pltpu

F32 = jnp.float32
BF16 = jnp.bfloat16

D_MODEL = 2048
DEPTH = 4
CONV_WIDTH = 4096
CONV_KERNEL = 31
CONV_HALO = 16
N_HEADS = 16
QK_NOPE_DIM = 128
QK_ROPE_DIM = 64
HALF_ROPE = QK_ROPE_DIM // 2
QK_DIM = QK_NOPE_DIM + QK_ROPE_DIM
V_HEAD_DIM = 128
Q_LORA_RANK = 512
KV_LORA_RANK = 512
MLA_WIDTH = N_HEADS * V_HEAD_DIM
ROPE_THETA = 10000.0
EPS = 1e-6
LANES = 128
MOD_ROWS = 8
VMEM_LIMIT_BYTES = 56 * 1024 * 1024
Q_PRESCALE = math.log2(math.e) / math.sqrt(QK_DIM)
NT_DIMS = (((1,), (1,)), ((), ()))
ATTN_LOOKAHEAD = 2
ATTN_BUFS = ATTN_LOOKAHEAD + 2
ATTN_KEY_CHUNK = 1024


def _params(*sem):
    return pltpu.CompilerParams(dimension_semantics=sem, vmem_limit_bytes=VMEM_LIMIT_BYTES)


def _silu(v):
    return v * jax.nn.sigmoid(v)


def _ada_body(c_ref, w_ref, b_ref, o_ref):
    c = c_ref[...]
    ca = _silu(c).astype(BF16)
    o_ref[0] = jnp.dot(ca, w_ref[0].astype(BF16), preferred_element_type=F32) + b_ref[0]


def _ada_mod(c_all, ada_w, ada_b):
    n = 3 * D_MODEL
    tn = 1024
    return pl.pallas_call(
        _ada_body,
        grid=(DEPTH, n // tn),
        in_specs=[pl.BlockSpec((MOD_ROWS, D_MODEL), lambda i, j: (0, 0)),
                  pl.BlockSpec((1, D_MODEL, tn), lambda i, j: (i, 0, j)),
                  pl.BlockSpec((1, 1, tn), lambda i, j: (i, 0, j))],
        out_specs=pl.BlockSpec((1, MOD_ROWS, tn), lambda i, j: (i, 0, j)),
        out_shape=jax.ShapeDtypeStruct((DEPTH, MOD_ROWS, n), F32),
        compiler_params=_params("parallel", "parallel"),
        name="ada_mod",
    )(c_all, ada_w, ada_b.reshape(DEPTH, 1, n))


def _modnorm_to(x_ref, g_ref, sc_ref, sh_ref, h_ref, tm, chunk):
    g = g_ref[...]
    sc = 1.0 + sc_ref[...]
    sh = sh_ref[...]

    def body(c, carry):
        r = pl.multiple_of(c * chunk, chunk)
        x = x_ref[pl.ds(r, chunk), :]
        ms = jnp.mean(x * x, axis=-1, keepdims=True)
        y = x * lax.rsqrt(ms + EPS) * g
        h_ref[pl.ds(r, chunk), :] = (y * sc + sh).astype(BF16)
        return carry

    lax.fori_loop(0, tm // chunk, body, 0)


def _conv_in_body(x_ref, g_ref, sc_ref, sh_ref, wa_ref, wg_ref, wz_ref, y_ref, z_ref, h_ref, *, tm):
    @pl.when(pl.program_id(2) == 0)
    def _():
        _modnorm_to(x_ref, g_ref, sc_ref, sh_ref, h_ref, tm, min(tm, 64))

    h = h_ref[...]
    a = jnp.dot(h, wa_ref[...], preferred_element_type=F32)
    gl = jnp.dot(h, wg_ref[...], preferred_element_type=F32)
    y_ref[...] = a * jax.nn.sigmoid(gl)
    z = jnp.dot(h, wz_ref[...], preferred_element_type=F32)
    z_ref[...] = _silu(z)


def _conv_in(x, g, scale, shift, w_in):
    b, l, _ = x.shape
    c = CONV_WIDTH
    tm = min(1024, l)
    tn = 512
    nj = c // tn
    return pl.pallas_call(
        functools.partial(_conv_in_body, tm=tm),
        grid=(b, l // tm, nj),
        in_specs=[pl.BlockSpec((None, tm, D_MODEL), lambda bi, i, j: (bi, i, 0)),
                  pl.BlockSpec((1, D_MODEL), lambda bi, i, j: (0, 0)),
                  pl.BlockSpec((None, 1, D_MODEL), lambda bi, i, j: (bi, 0, 0)),
                  pl.BlockSpec((None, 1, D_MODEL), lambda bi, i, j: (bi, 0, 0)),
                  pl.BlockSpec((D_MODEL, tn), lambda bi, i, j: (0, j)),
                  pl.BlockSpec((D_MODEL, tn), lambda bi, i, j: (0, j + nj)),
                  pl.BlockSpec((D_MODEL, tn), lambda bi, i, j: (0, j + 2 * nj))],
        out_specs=[pl.BlockSpec((None, tm, tn), lambda bi, i, j: (bi, i, j)),
                   pl.BlockSpec((None, tm, tn), lambda bi, i, j: (bi, i, j))],
        out_shape=[jax.ShapeDtypeStruct((b, l, c), F32), jax.ShapeDtypeStruct((b, l, c), F32)],
        scratch_shapes=[pltpu.VMEM((tm, D_MODEL), BF16)],
        compiler_params=_params("parallel", "parallel", "arbitrary"),
        name="conv_in",
    )(x, g, scale, shift, w_in, w_in, w_in)


def _dwconv_body(yp_ref, ym_ref, yn_ref, zs_ref, w_ref, b_ref, lg_ref, lb_ref, o_ref,
                 xbuf, shbuf, cbuf, *, tl, strip, ln_rows):
    i = pl.program_id(1)
    n = pl.num_programs(1)
    halo = CONV_HALO
    xbuf[0:halo, :] = jnp.where(i > 0, yp_ref[...], 0.0)
    xbuf[halo:halo + tl, :] = ym_ref[...]
    xbuf[halo + tl:2 * halo + tl, :] = jnp.where(i < n - 1, yn_ref[...], 0.0)

    first = halo - CONV_KERNEL // 2
    sh_rows = tl + 24

    def cb_body(cb, carry):
        lanes = pl.ds(pl.multiple_of(cb * LANES, LANES), LANES)
        for r in range(1, 8):
            shbuf[r, :, :] = xbuf[pl.ds(r, sh_rows), lanes]
        bias = b_ref[:, lanes]
        for s in range(tl // strip):
            acc = jnp.broadcast_to(bias, (strip, LANES))
            for k in range(CONV_KERNEL):
                a8, r = divmod(first + k, 8)
                start = s * strip + 8 * a8
                if r == 0:
                    win = xbuf[pl.ds(start, strip), lanes]
                else:
                    win = shbuf[r, pl.ds(start, strip), :]
                acc = acc + win * w_ref[k:k + 1, lanes]
            cbuf[pl.ds(s * strip, strip), lanes] = acc
        return carry

    lax.fori_loop(0, CONV_WIDTH // LANES, cb_body, 0)

    lg = lg_ref[...]
    lb = lb_ref[...]

    def ln_body(t, carry):
        rows = pl.ds(pl.multiple_of(t * ln_rows, ln_rows), ln_rows)
        v = cbuf[rows, :]
        mu = jnp.mean(v, axis=-1, keepdims=True)
        vc = v - mu
        var = jnp.mean(vc * vc, axis=-1, keepdims=True)
        yn = vc * lax.rsqrt(var + EPS) * lg + lb
        o_ref[rows, :] = (_silu(yn) * zs_ref[rows, :]).astype(BF16)
        return carry

    lax.fori_loop(0, tl // ln_rows, ln_body, 0)


def _dwconv(y, zs, dw_w, dw_b, ln_g, ln_b):
    b, l, c = y.shape
    tl = min(256, l)
    hb = tl // CONV_HALO
    nhb = l // CONV_HALO
    body = functools.partial(_dwconv_body, tl=tl, strip=min(64, tl), ln_rows=min(32, tl))
    row = lambda bi, i: (0, 0)
    return pl.pallas_call(
        body,
        grid=(b, l // tl),
        in_specs=[pl.BlockSpec((None, CONV_HALO, c), lambda bi, i: (bi, jnp.maximum(i * hb - 1, 0), 0)),
                  pl.BlockSpec((None, tl, c), lambda bi, i: (bi, i, 0)),
                  pl.BlockSpec((None, CONV_HALO, c), lambda bi, i: (bi, jnp.minimum((i + 1) * hb, nhb - 1), 0)),
                  pl.BlockSpec((None, tl, c), lambda bi, i: (bi, i, 0)),
                  pl.BlockSpec((CONV_KERNEL, c), row),
                  pl.BlockSpec((1, c), row),
                  pl.BlockSpec((1, c), row),
                  pl.BlockSpec((1, c), row)],
        out_specs=pl.BlockSpec((None, tl, c), lambda bi, i: (bi, i, 0)),
        out_shape=jax.ShapeDtypeStruct((b, l, c), BF16),
        scratch_shapes=[pltpu.VMEM((tl + 2 * CONV_HALO, c), F32),
                        pltpu.VMEM((8, tl + 24, LANES), F32),
                        pltpu.VMEM((tl, c), F32)],
        compiler_params=_params("parallel", "parallel"),
        name="dwconv_ln",
    )(y, y, y, zs, dw_w, dw_b.reshape(1, c), ln_g.reshape(1, c), ln_b.reshape(1, c))


def _out_proj_body(a_ref, w_ref, x_ref, gt_ref, o_ref):
    y = jnp.dot(a_ref[...], w_ref[...], preferred_element_type=F32)
    o_ref[...] = x_ref[...] + gt_ref[...] * y


def _out_proj_norm_body(a_ref, w_ref, x_ref, gt_ref, g_ref, o_ref):
    y = jnp.dot(a_ref[...], w_ref[...], preferred_element_type=F32)
    xn = x_ref[...] + gt_ref[...] * y
    ms = jnp.mean(xn * xn, axis=-1, keepdims=True)
    o_ref[...] = xn * lax.rsqrt(ms + EPS) * g_ref[...]


def _out_proj(a, w, x, gate, final_g=None):
    b, l, k = a.shape
    if final_g is None:
        tm, tn, body, extra, extra_specs = min(1024, l), 512, _out_proj_body, (), []
    else:
        tm, tn, body, extra = min(512, l), D_MODEL, _out_proj_norm_body, (final_g,)
        extra_specs = [pl.BlockSpec((1, D_MODEL), lambda bi, i, j: (0, 0))]
    return pl.pallas_call(
        body,
        grid=(b, l // tm, D_MODEL // tn),
        in_specs=[pl.BlockSpec((None, tm, k), lambda bi, i, j: (bi, i, 0)),
                  pl.BlockSpec((k, tn), lambda bi, i, j: (0, j)),
                  pl.BlockSpec((None, tm, tn), lambda bi, i, j: (bi, i, j)),
                  pl.BlockSpec((None, 1, tn), lambda bi, i, j: (bi, 0, j))] + extra_specs,
        out_specs=pl.BlockSpec((None, tm, tn), lambda bi, i, j: (bi, i, j)),
        out_shape=jax.ShapeDtypeStruct((b, l, D_MODEL), F32),
        compiler_params=_params("parallel", "parallel", "parallel"),
        name="out_proj",
    )(a, w, x, gate, *extra)


def _mla_in_body(x_ref, g_ref, sc_ref, sh_ref, wq_ref, wkv_ref, wka_ref, wkb_ref, wz_ref,
                 qn_ref, kvn_ref, cc_ref, ss_ref,
                 cq_out, ckv_out, kr_out, zs_out, h_ref, *, tm, zchunk):
    _modnorm_to(x_ref, g_ref, sc_ref, sh_ref, h_ref, tm, min(tm, 64))
    h = h_ref[...]

    def rms(v, gain):
        ms = jnp.mean(v * v, axis=-1, keepdims=True)
        return (v * lax.rsqrt(ms + EPS) * gain).astype(BF16)

    cq_out[...] = rms(jnp.dot(h, wq_ref[...], preferred_element_type=F32), qn_ref[...])
    ckv_out[...] = rms(jnp.dot(h, wkv_ref[...], preferred_element_type=F32), kvn_ref[...])
    ka = jnp.dot(h, wka_ref[...], preferred_element_type=F32)
    kb = jnp.dot(h, wkb_ref[...], preferred_element_type=F32)
    kr = ka * cc_ref[...] + kb * ss_ref[...]
    kr_out[...] = kr[:, :QK_ROPE_DIM].astype(BF16)
    for c in range(MLA_WIDTH // zchunk):
        z = jnp.dot(h, wz_ref[:, c * zchunk:(c + 1) * zchunk], preferred_element_type=F32)
        zs_out[:, c * zchunk:(c + 1) * zchunk] = _silu(z)


def _mla_in(x, g, scale, shift, wq, wkv, wka, wkb, wz, q_norm, kv_norm, cc, ss):
    b, l, _ = x.shape
    tm = min(256, l)
    const = lambda bi, i: (0, 0)
    rows = lambda bi, i: (bi, i, 0)
    per_b = lambda bi, i: (bi, 0, 0)
    return pl.pallas_call(
        functools.partial(_mla_in_body, tm=tm, zchunk=512),
        grid=(b, l // tm),
        in_specs=[pl.BlockSpec((None, tm, D_MODEL), rows),
                  pl.BlockSpec((1, D_MODEL), const),
                  pl.BlockSpec((None, 1, D_MODEL), per_b),
                  pl.BlockSpec((None, 1, D_MODEL), per_b),
                  pl.BlockSpec((D_MODEL, Q_LORA_RANK), const),
                  pl.BlockSpec((D_MODEL, KV_LORA_RANK), const),
                  pl.BlockSpec((D_MODEL, LANES), const),
                  pl.BlockSpec((D_MODEL, LANES), const),
                  pl.BlockSpec((D_MODEL, MLA_WIDTH), const),
                  pl.BlockSpec((1, Q_LORA_RANK), const),
                  pl.BlockSpec((1, KV_LORA_RANK), const),
                  pl.BlockSpec((tm, LANES), lambda bi, i: (i, 0)),
                  pl.BlockSpec((tm, LANES), lambda bi, i: (i, 0))],
        out_specs=[pl.BlockSpec((None, tm, Q_LORA_RANK), rows),
                   pl.BlockSpec((None, tm, KV_LORA_RANK), rows),
                   pl.BlockSpec((None, tm, QK_ROPE_DIM), rows),
                   pl.BlockSpec((None, tm, MLA_WIDTH), rows)],
        out_shape=[jax.ShapeDtypeStruct((b, l, Q_LORA_RANK), BF16),
                   jax.ShapeDtypeStruct((b, l, KV_LORA_RANK), BF16),
                   jax.ShapeDtypeStruct((b, l, QK_ROPE_DIM), BF16),
                   jax.ShapeDtypeStruct((b, l, MLA_WIDTH), F32)],
        scratch_shapes=[pltpu.VMEM((tm, D_MODEL), BF16)],
        compiler_params=_params("parallel", "parallel"),
        name="mla_in",
    )(x, g, scale, shift, wq, wkv, wka, wkb, wz, q_norm, kv_norm, cc, ss)


def _mla_up_body(cq_ref, ckv_ref, kr_ref, cost_ref, sint_ref, wuqt_ref, wuk_ref, wuvt_ref,
                 qt_out, k_out, vt_out):
    cq = cq_ref[...]
    ckv = ckv_ref[...]
    cost = cost_ref[...]
    sint = sint_ref[...]
    qt = lax.dot_general(wuqt_ref[...], cq, NT_DIMS, preferred_element_type=F32)
    for h in range(N_HEADS):
        r0 = h * QK_DIM
        r1 = r0 + QK_NOPE_DIM
        r2 = r1 + HALF_ROPE
        r3 = r2 + HALF_ROPE
        x1 = qt[r1:r2]
        x2 = qt[r2:r3]
        qt_out[r0:r1, :] = (qt[r0:r1] * Q_PRESCALE).astype(BF16)
        qt_out[r1:r2, :] = ((x1 * cost - x2 * sint) * Q_PRESCALE).astype(BF16)
        qt_out[r2:r3, :] = ((x2 * cost + x1 * sint) * Q_PRESCALE).astype(BF16)
    kn = jnp.dot(ckv, wuk_ref[...], preferred_element_type=F32)
    kr = kr_ref[...]
    for h in range(N_HEADS):
        k_out[h, :, 0:QK_NOPE_DIM] = kn[:, h * QK_NOPE_DIM:(h + 1) * QK_NOPE_DIM].astype(BF16)
        k_out[h, :, QK_NOPE_DIM:QK_DIM] = kr
    vt = lax.dot_general(wuvt_ref[...], ckv, NT_DIMS, preferred_element_type=F32)
    vt_out[...] = vt.astype(BF16)


def _mla_up(cq, ckv, kr, cost, sint, wuqt, wuk, wuvt):
    b, l, _ = cq.shape
    tm = min(256, l)
    const = lambda bi, i: (0, 0)
    rows = lambda bi, i: (bi, i, 0)
    cols = lambda bi, i: (bi, 0, i)
    return pl.pallas_call(
        _mla_up_body,
        grid=(b, l // tm),
        in_specs=[pl.BlockSpec((None, tm, Q_LORA_RANK), rows),
                  pl.BlockSpec((None, tm, KV_LORA_RANK), rows),
                  pl.BlockSpec((None, tm, QK_ROPE_DIM), rows),
                  pl.BlockSpec((HALF_ROPE, tm), lambda bi, i: (0, i)),
                  pl.BlockSpec((HALF_ROPE, tm), lambda bi, i: (0, i)),
                  pl.BlockSpec((N_HEADS * QK_DIM, Q_LORA_RANK), const),
                  pl.BlockSpec((KV_LORA_RANK, N_HEADS * QK_NOPE_DIM), const),
                  pl.BlockSpec((MLA_WIDTH, KV_LORA_RANK), const)],
        out_specs=[pl.BlockSpec((None, N_HEADS * QK_DIM, tm), cols),
                   pl.BlockSpec((None, N_HEADS, tm, QK_DIM), lambda bi, i: (bi, 0, i, 0)),
                   pl.BlockSpec((None, MLA_WIDTH, tm), cols)],
        out_shape=[jax.ShapeDtypeStruct((b, N_HEADS * QK_DIM, l), BF16),
                   jax.ShapeDtypeStruct((b, N_HEADS, l, QK_DIM), BF16),
                   jax.ShapeDtypeStruct((b, MLA_WIDTH, l), BF16)],
        compiler_params=_params("parallel", "parallel"),
        name="mla_up",
    )(cq, ckv, kr, cost, sint, wuqt, wuk, wuvt)


def _attn_body(qt_ref, k_ref, vt_ref, zs_ref, o_ref, acc_ref, m_ref, l_ref, *stage_refs, nchunks, tk):
    bufs = tuple(stage_refs[4 * u:4 * u + 4] for u in range(ATTN_BUFS))

    def scores(c, s_ref, cm_ref):
        r = c * tk if isinstance(c, int) else pl.multiple_of(c * tk, tk)
        s = jnp.dot(k_ref[pl.ds(r, tk), :], qt_ref[...], preferred_element_type=F32)
        s_ref[...] = s
        cm_ref[...] = jnp.max(s, axis=0, keepdims=True)

    def softmax(s_ref, cm_ref, p_ref, al_ref):
        m_prev = m_ref[...]
        m_new = jnp.maximum(m_prev, cm_ref[...])
        alpha = jnp.exp2(m_prev - m_new)
        p = jnp.exp2(s_ref[...] - m_new)
        l_ref[...] = alpha * l_ref[...] + jnp.sum(p, axis=0, keepdims=True)
        p_ref[...] = p.astype(BF16)
        al_ref[...] = alpha
        m_ref[...] = m_new

    def values(c, p_ref, al_ref):
        r = c * tk if isinstance(c, int) else pl.multiple_of(c * tk, tk)
        pv = jnp.dot(vt_ref[:, pl.ds(r, tk)], p_ref[...], preferred_element_type=F32)
        acc_ref[...] = al_ref[...] * acc_ref[...] + pv

    m_ref[...] = jnp.full(m_ref.shape, -jnp.inf, F32)
    l_ref[...] = jnp.zeros(l_ref.shape, F32)
    acc_ref[...] = jnp.zeros(acc_ref.shape, F32)
    for c in range(min(ATTN_LOOKAHEAD, nchunks)):
        scores(c, *bufs[c % ATTN_BUFS][:2])
    for c in range(nchunks):
        s_cur, cm_cur, p_cur, al_cur = bufs[c % ATTN_BUFS]
        softmax(s_cur, cm_cur, p_cur, al_cur)
        if c > 0:
            values(c - 1, *bufs[(c - 1) % ATTN_BUFS][2:])
        if c + ATTN_LOOKAHEAD < nchunks:
            scores(c + ATTN_LOOKAHEAD, *bufs[(c + ATTN_LOOKAHEAD) % ATTN_BUFS][:2])
    values(nchunks - 1, *bufs[(nchunks - 1) % ATTN_BUFS][2:])
    o = acc_ref[...] / l_ref[...]
    o_ref[...] = (o.T * zs_ref[...]).astype(BF16)


def _attention(qt, k, vt, zs):
    b, _, l = qt.shape
    tq = min(512, l)
    tk = min(ATTN_KEY_CHUNK, l)
    nchunks = l // tk
    stat = pltpu.VMEM((1, tq), F32)
    return pl.pallas_call(
        functools.partial(_attn_body, nchunks=nchunks, tk=tk),
        grid=(b, N_HEADS, l // tq),
        in_specs=[pl.BlockSpec((None, QK_DIM, tq), lambda bi, h, qi: (bi, h, qi)),
                  pl.BlockSpec((None, None, l, QK_DIM), lambda bi, h, qi: (bi, h, 0, 0)),
                  pl.BlockSpec((None, V_HEAD_DIM, l), lambda bi, h, qi: (bi, h, 0)),
                  pl.BlockSpec((None, tq, V_HEAD_DIM), lambda bi, h, qi: (bi, qi, h))],
        out_specs=pl.BlockSpec((None, tq, V_HEAD_DIM), lambda bi, h, qi: (bi, qi, h)),
        out_shape=jax.ShapeDtypeStruct((b, l, MLA_WIDTH), BF16),
        scratch_shapes=[pltpu.VMEM((V_HEAD_DIM, tq), F32), stat, stat]
        + [pltpu.VMEM((tk, tq), F32), stat, pltpu.VMEM((tk, tq), BF16), stat] * ATTN_BUFS,
        compiler_params=_params("parallel", "parallel", "parallel"),
        name="mla_attn",
    )(qt, k, vt, zs)


def _rope_tables(length):
    inv = 1.0 / (ROPE_THETA ** (jnp.arange(0, QK_ROPE_DIM, 2, dtype=F32) / QK_ROPE_DIM))
    ang = jnp.arange(length, dtype=F32)[:, None] * inv[None, :]
    return jnp.cos(ang), jnp.sin(ang)


def _prep_weights(conv_w_in, conv_w_out, mla_w_in, mla_w_uq, mla_w_ukv, mla_w_out):
    q0, q1, q2 = Q_LORA_RANK, Q_LORA_RANK + KV_LORA_RANK, Q_LORA_RANK + KV_LORA_RANK + QK_ROPE_DIM
    pad = ((0, 0), (0, 0), (0, LANES - QK_ROPE_DIM))
    wkr = mla_w_in[:, :, q1:q2]
    wkr_swapped = jnp.concatenate([wkr[..., HALF_ROPE:], wkr[..., :HALF_ROPE]], axis=-1)
    n_mla = mla_w_ukv.shape[0]
    wukv = mla_w_ukv.reshape(n_mla, KV_LORA_RANK, N_HEADS, QK_NOPE_DIM + V_HEAD_DIM)
    return dict(
        conv_w_in=conv_w_in.astype(BF16),
        conv_w_out=conv_w_out.astype(BF16),
        wq=mla_w_in[:, :, :q0].astype(BF16),
        wkv=mla_w_in[:, :, q0:q1].astype(BF16),
        wka=jnp.pad(wkr, pad).astype(BF16),
        wkb=jnp.pad(wkr_swapped, pad).astype(BF16),
        wz=mla_w_in[:, :, q2:].astype(BF16),
        wuqt=jnp.swapaxes(mla_w_uq, 1, 2).astype(BF16),
        wuk=wukv[..., :QK_NOPE_DIM].reshape(n_mla, KV_LORA_RANK, N_HEADS * QK_NOPE_DIM).astype(BF16),
        wuvt=jnp.swapaxes(wukv[..., QK_NOPE_DIM:].reshape(n_mla, KV_LORA_RANK, MLA_WIDTH), 1, 2).astype(BF16),
        mla_w_out=mla_w_out.astype(BF16),
    )


def _trunk(x, mod, norm_g, wts, conv_dw_w, conv_dw_b, conv_ln_g, conv_ln_b,
           mla_q_norm, mla_kv_norm, final_g):
    assert DEPTH % 2 == 0
    length = x.shape[1]
    cos, sin = _rope_tables(length)
    lane_pad = ((0, 0), (0, LANES - QK_ROPE_DIM))
    cc = jnp.pad(jnp.concatenate([cos, cos], axis=-1), lane_pad)
    ss = jnp.pad(jnp.concatenate([-sin, sin], axis=-1), lane_pad)
    cost, sint = cos.T, sin.T
    for i in range(DEPTH):
        shift = mod[i, :, 0][:, None, :]
        scale = mod[i, :, 1][:, None, :]
        gate = mod[i, :, 2][:, None, :]
        g = norm_g[i][None, :]
        j = i // 2
        if i % 2 == 0:
            y, zs = _conv_in(x, g, scale, shift, wts["conv_w_in"][j])
            yc = _dwconv(y, zs, conv_dw_w[j], conv_dw_b[j], conv_ln_g[j], conv_ln_b[j])
            x = _out_proj(yc, wts["conv_w_out"][j], x, gate)
        else:
            cq, ckv, kr, zs = _mla_in(x, g, scale, shift, wts["wq"][j], wts["wkv"][j], wts["wka"][j],
                                      wts["wkb"][j], wts["wz"][j], mla_q_norm[j][None, :],
                                      mla_kv_norm[j][None, :], cc, ss)
            qt, k, vt = _mla_up(cq, ckv, kr, cost, sint, wts["wuqt"][j], wts["wuk"][j], wts["wuvt"][j])
            og = _attention(qt, k, vt, zs)
            x = _out_proj(og, wts["mla_w_out"][j], x, gate, final_g[None, :] if i == DEPTH - 1 else None)
    return x


def kernel(x_prompt, x_sample, c_prompt, c_sample, norm_g, ada_w, ada_b, conv_w_in, conv_dw_w, conv_dw_b,
           conv_ln_g, conv_ln_b, conv_w_out, mla_w_in, mla_q_norm, mla_kv_norm, mla_w_uq, mla_w_ukv,
           mla_w_out, final_g):
    nb_p, nb_s = c_prompt.shape[0], c_sample.shape[0]
    c_all = jnp.concatenate([c_prompt, c_sample, jnp.zeros((MOD_ROWS - nb_p - nb_s, D_MODEL), F32)], axis=0)
    mod = _ada_mod(c_all, ada_w, ada_b).reshape(DEPTH, MOD_ROWS, 3, D_MODEL)
    wts = _prep_weights(conv_w_in, conv_w_out, mla_w_in, mla_w_uq, mla_w_ukv, mla_w_out)
    rest = (norm_g, wts, conv_dw_w, conv_dw_b, conv_ln_g, conv_ln_b, mla_q_norm, mla_kv_norm, final_g)
    y_prompt = _trunk(x_prompt, mod[:, :nb_p], *rest)
    y_sample = _trunk(x_sample, mod[:, nb_p:nb_p + nb_s], *rest)
    return (y_prompt, y_sample)
```

```python
import functools
import math

import jax
import jax.numpy as jnp
from jax import lax
from jax.experimental import pallas as pl
from jax.experimental.pallas import tpu as pltpu

F32 = jnp.float32
BF16 = jnp.bfloat16

D_MODEL = 2048
DEPTH = 4
CONV_WIDTH = 4096
CONV_KERNEL = 31
CONV_HALO = 16
N_HEADS = 16
QK_NOPE_DIM = 128
QK_ROPE_DIM = 64
HALF_ROPE = QK_ROPE_DIM // 2
QK_DIM = QK_NOPE_DIM + QK_ROPE_DIM
V_HEAD_DIM = 128
Q_LORA_RANK = 512
KV_LORA_RANK = 512
MLA_WIDTH = N_HEADS * V_HEAD_DIM
ROPE_THETA = 10000.0
EPS = 1e-6
LANES = 128
SUBLANES = 8
MOD_ROWS = SUBLANES
VMEM_LIMIT_BYTES = 56 * 1024 * 1024
Q_PRESCALE = math.log2(math.e) / math.sqrt(QK_DIM)
NT_DIMS = (((1,), (1,)), ((), ()))
ATTN_LOOKAHEAD = 2
ATTN_BUFS = ATTN_LOOKAHEAD + 2
ATTN_KEY_CHUNK = 1024


def _params(*sem):
    return pltpu.CompilerParams(dimension_semantics=sem, vmem_limit_bytes=VMEM_LIMIT_BYTES)


def _silu(v):
    return v * jax.nn.sigmoid(v)


def _ada_body(c_ref, w_ref, b_ref, o_ref):
    c = c_ref[...]
    ca = _silu(c).astype(BF16)
    o_ref[0] = jnp.dot(ca, w_ref[0].astype(BF16), preferred_element_type=F32) + b_ref[0]


def _ada_mod(c_all, ada_w, ada_b):
    n = 3 * D_MODEL
    tn = 1024
    return pl.pallas_call(
        _ada_body,
        grid=(DEPTH, n // tn),
        in_specs=[pl.BlockSpec((MOD_ROWS, D_MODEL), lambda i, j: (0, 0)),
                  pl.BlockSpec((1, D_MODEL, tn), lambda i, j: (i, 0, j)),
                  pl.BlockSpec((1, 1, tn), lambda i, j: (i, 0, j))],
        out_specs=pl.BlockSpec((1, MOD_ROWS, tn), lambda i, j: (i, 0, j)),
        out_shape=jax.ShapeDtypeStruct((DEPTH, MOD_ROWS, n), F32),
        compiler_params=_params("parallel", "parallel"),
        name="ada_mod",
    )(c_all, ada_w, ada_b.reshape(DEPTH, 1, n))


def _modnorm_to(x_ref, g_ref, sc_ref, sh_ref, h_ref, tm, chunk):
    g = g_ref[...]
    sc = 1.0 + sc_ref[...]
    sh = sh_ref[...]

    def body(c, carry):
        r = pl.multiple_of(c * chunk, chunk)
        x = x_ref[pl.ds(r, chunk), :]
        ms = jnp.mean(x * x, axis=-1, keepdims=True)
        y = x * lax.rsqrt(ms + EPS) * g
        h_ref[pl.ds(r, chunk), :] = (y * sc + sh).astype(BF16)
        return carry

    lax.fori_loop(0, tm // chunk, body, 0)


def _conv_in_body(x_ref, g_ref, sc_ref, sh_ref, wa_ref, wg_ref, wz_ref, y_ref, z_ref, h_ref, *, tm):
    @pl.when(pl.program_id(2) == 0)
    def _():
        _modnorm_to(x_ref, g_ref, sc_ref, sh_ref, h_ref, tm, min(tm, 64))

    h = h_ref[...]
    a = jnp.dot(h, wa_ref[...], preferred_element_type=F32)
    gl = jnp.dot(h, wg_ref[...], preferred_element_type=F32)
    y_ref[...] = a * jax.nn.sigmoid(gl)
    z = jnp.dot(h, wz_ref[...], preferred_element_type=F32)
    z_ref[...] = _silu(z)


def _conv_in(x, g, scale, shift, w_in):
    b, l, _ = x.shape
    c = CONV_WIDTH
    tm = min(1024, l)
    tn = 512
    nj = c // tn
    return pl.pallas_call(
        functools.partial(_conv_in_body, tm=tm),
        grid=(b, l // tm, nj),
        in_specs=[pl.BlockSpec((None, tm, D_MODEL), lambda bi, i, j: (bi, i, 0)),
                  pl.BlockSpec((1, D_MODEL), lambda bi, i, j: (0, 0)),
                  pl.BlockSpec((None, 1, D_MODEL), lambda bi, i, j: (bi, 0, 0)),
                  pl.BlockSpec((None, 1, D_MODEL), lambda bi, i, j: (bi, 0, 0)),
                  pl.BlockSpec((D_MODEL, tn), lambda bi, i, j: (0, j)),
                  pl.BlockSpec((D_MODEL, tn), lambda bi, i, j: (0, j + nj)),
                  pl.BlockSpec((D_MODEL, tn), lambda bi, i, j: (0, j + 2 * nj))],
        out_specs=[pl.BlockSpec((None, tm, tn), lambda bi, i, j: (bi, i, j)),
                   pl.BlockSpec((None, tm, tn), lambda bi, i, j: (bi, i, j))],
        out_shape=[jax.ShapeDtypeStruct((b, l, c), F32), jax.ShapeDtypeStruct((b, l, c), F32)],
        scratch_shapes=[pltpu.VMEM((tm, D_MODEL), BF16)],
        compiler_params=_params("parallel", "parallel", "arbitrary"),
        name="conv_in",
    )(x, g, scale, shift, w_in, w_in, w_in)


def _dwconv_body(yp_ref, ym_ref, yn_ref, zs_ref, w_ref, b_ref, lg_ref, lb_ref, o_ref,
                 xbuf, shbuf, cbuf, *, tl, strip, ln_rows):
    i = pl.program_id(1)
    n = pl.num_programs(1)
    halo = CONV_HALO
    xbuf[0:halo, :] = jnp.where(i > 0, yp_ref[...], 0.0)
    xbuf[halo:halo + tl, :] = ym_ref[...]
    xbuf[halo + tl:2 * halo + tl, :] = jnp.where(i < n - 1, yn_ref[...], 0.0)

    first = halo - CONV_KERNEL // 2
    sh_rows = tl + 2 * halo - SUBLANES

    def cb_body(cb, carry):
        lanes = pl.ds(pl.multiple_of(cb * LANES, LANES), LANES)
        for r in range(1, SUBLANES):
            shbuf[r, :, :] = xbuf[pl.ds(r, sh_rows), lanes]
        bias = b_ref[:, lanes]
        for s in range(tl // strip):
            acc = jnp.broadcast_to(bias, (strip, LANES))
            for k in range(CONV_KERNEL):
                a8, r = divmod(first + k, SUBLANES)
                start = s * strip + SUBLANES * a8
                if r == 0:
                    win = xbuf[pl.ds(start, strip), lanes]
                else:
                    win = shbuf[r, pl.ds(start, strip), :]
                acc = acc + win * w_ref[k:k + 1, lanes]
            cbuf[pl.ds(s * strip, strip), lanes] = acc
        return carry

    lax.fori_loop(0, CONV_WIDTH // LANES, cb_body, 0)

    lg = lg_ref[...]
    lb = lb_ref[...]

    def ln_body(t, carry):
        rows = pl.ds(pl.multiple_of(t * ln_rows, ln_rows), ln_rows)
        v = cbuf[rows, :]
        mu = jnp.mean(v, axis=-1, keepdims=True)
        vc = v - mu
        var = jnp.mean(vc * vc, axis=-1, keepdims=True)
        yn = vc * lax.rsqrt(var + EPS) * lg + lb
        o_ref[rows, :] = (_silu(yn) * zs_ref[rows, :]).astype(BF16)
        return carry

    lax.fori_loop(0, tl // ln_rows, ln_body, 0)


def _dwconv(y, zs, dw_w, dw_b, ln_g, ln_b):
    b, l, c = y.shape
    tl = min(256, l)
    hb = tl // CONV_HALO
    nhb = l // CONV_HALO
    body = functools.partial(_dwconv_body, tl=tl, strip=min(64, tl), ln_rows=min(64, tl))
    row = lambda bi, i: (0, 0)
    return pl.pallas_call(
        body,
        grid=(b, l // tl),
        in_specs=[pl.BlockSpec((None, CONV_HALO, c), lambda bi, i: (bi, jnp.maximum(i * hb - 1, 0), 0)),
                  pl.BlockSpec((None, tl, c), lambda bi, i: (bi, i, 0)),
                  pl.BlockSpec((None, CONV_HALO, c), lambda bi, i: (bi, jnp.minimum((i + 1) * hb, nhb - 1), 0)),
                  pl.BlockSpec((None, tl, c), lambda bi, i: (bi, i, 0)),
                  pl.BlockSpec((CONV_KERNEL, c), row),
                  pl.BlockSpec((1, c), row),
                  pl.BlockSpec((1, c), row),
                  pl.BlockSpec((1, c), row)],
        out_specs=pl.BlockSpec((None, tl, c), lambda bi, i: (bi, i, 0)),
        out_shape=jax.ShapeDtypeStruct((b, l, c), BF16),
        scratch_shapes=[pltpu.VMEM((tl + 2 * CONV_HALO, c), F32),
                        pltpu.VMEM((SUBLANES, tl + 2 * CONV_HALO - SUBLANES, LANES), F32),
                        pltpu.VMEM((tl, c), F32)],
        compiler_params=_params("parallel", "parallel"),
        name="dwconv_ln",
    )(y, y, y, zs, dw_w, dw_b.reshape(1, c), ln_g.reshape(1, c), ln_b.reshape(1, c))


def _out_proj_body(a_ref, w_ref, x_ref, gt_ref, o_ref):
    y = jnp.dot(a_ref[...], w_ref[...], preferred_element_type=F32)
    o_ref[...] = x_ref[...] + gt_ref[...] * y


def _out_proj_norm_body(a_ref, w_ref, x_ref, gt_ref, g_ref, o_ref):
    y = jnp.dot(a_ref[...], w_ref[...], preferred_element_type=F32)
    xn = x_ref[...] + gt_ref[...] * y
    ms = jnp.mean(xn * xn, axis=-1, keepdims=True)
    o_ref[...] = xn * lax.rsqrt(ms + EPS) * g_ref[...]


def _out_proj(a, w, x, gate, final_g=None):
    b, l, k = a.shape
    if final_g is None:
        tm, tn, body, extra, extra_specs = min(1024, l), 512, _out_proj_body, (), []
    else:
        tm, tn, body, extra = min(512, l), D_MODEL, _out_proj_norm_body, (final_g,)
        extra_specs = [pl.BlockSpec((1, D_MODEL), lambda bi, i, j: (0, 0))]
    return pl.pallas_call(
        body,
        grid=(b, l // tm, D_MODEL // tn),
        in_specs=[pl.BlockSpec((None, tm, k), lambda bi, i, j: (bi, i, 0)),
                  pl.BlockSpec((k, tn), lambda bi, i, j: (0, j)),
                  pl.BlockSpec((None, tm, tn), lambda bi, i, j: (bi, i, j)),
                  pl.BlockSpec((None, 1, tn), lambda bi, i, j: (bi, 0, j))] + extra_specs,
        out_specs=pl.BlockSpec((None, tm, tn), lambda bi, i, j: (bi, i, j)),
        out_shape=jax.ShapeDtypeStruct((b, l, D_MODEL), F32),
        compiler_params=_params("parallel", "parallel", "parallel"),
        name="out_proj",
    )(a, w, x, gate, *extra)


def _mla_in_body(x_ref, g_ref, sc_ref, sh_ref, wq_ref, wkv_ref, wka_ref, wkb_ref, wz_ref,
                 qn_ref, kvn_ref, cc_ref, ss_ref,
                 cq_out, ckv_out, kr_out, zs_out, h_ref, *, tm, zchunk):
    _modnorm_to(x_ref, g_ref, sc_ref, sh_ref, h_ref, tm, min(tm, 64))
    h = h_ref[...]

    def rms(v, gain):
        ms = jnp.mean(v * v, axis=-1, keepdims=True)
        return (v * lax.rsqrt(ms + EPS) * gain).astype(BF16)

    cq_out[...] = rms(jnp.dot(h, wq_ref[...], preferred_element_type=F32), qn_ref[...])
    ckv_out[...] = rms(jnp.dot(h, wkv_ref[...], preferred_element_type=F32), kvn_ref[...])
    ka = jnp.dot(h, wka_ref[...], preferred_element_type=F32)
    kb = jnp.dot(h, wkb_ref[...], preferred_element_type=F32)
    kr = ka * cc_ref[...] + kb * ss_ref[...]
    kr_out[...] = kr[:, :QK_ROPE_DIM].astype(BF16)
    for c in range(MLA_WIDTH // zchunk):
        z = jnp.dot(h, wz_ref[:, c * zchunk:(c + 1) * zchunk], preferred_element_type=F32)
        zs_out[:, c * zchunk:(c + 1) * zchunk] = _silu(z)


def _mla_in(x, g, scale, shift, wq, wkv, wka, wkb, wz, q_norm, kv_norm, cc, ss):
    b, l, _ = x.shape
    tm = min(256, l)
    const = lambda bi, i: (0, 0)
    rows = lambda bi, i: (bi, i, 0)
    per_b = lambda bi, i: (bi, 0, 0)
    return pl.pallas_call(
        functools.partial(_mla_in_body, tm=tm, zchunk=512),
        grid=(b, l // tm),
        in_specs=[pl.BlockSpec((None, tm, D_MODEL), rows),
                  pl.BlockSpec((1, D_MODEL), const),
                  pl.BlockSpec((None, 1, D_MODEL), per_b),
                  pl.BlockSpec((None, 1, D_MODEL), per_b),
                  pl.BlockSpec((D_MODEL, Q_LORA_RANK), const),
                  pl.BlockSpec((D_MODEL, KV_LORA_RANK), const),
                  pl.BlockSpec((D_MODEL, LANES), const),
                  pl.BlockSpec((D_MODEL, LANES), const),
                  pl.BlockSpec((D_MODEL, MLA_WIDTH), const),
                  pl.BlockSpec((1, Q_LORA_RANK), const),
                  pl.BlockSpec((1, KV_LORA_RANK), const),
                  pl.BlockSpec((tm, LANES), lambda bi, i: (i, 0)),
                  pl.BlockSpec((tm, LANES), lambda bi, i: (i, 0))],
        out_specs=[pl.BlockSpec((None, tm, Q_LORA_RANK), rows),
                   pl.BlockSpec((None, tm, KV_LORA_RANK), rows),
                   pl.BlockSpec((None, tm, QK_ROPE_DIM), rows),
                   pl.BlockSpec((None, tm, MLA_WIDTH), rows)],
        out_shape=[jax.ShapeDtypeStruct((b, l, Q_LORA_RANK), BF16),
                   jax.ShapeDtypeStruct((b, l, KV_LORA_RANK), BF16),
                   jax.ShapeDtypeStruct((b, l, QK_ROPE_DIM), BF16),
                   jax.ShapeDtypeStruct((b, l, MLA_WIDTH), F32)],
        scratch_shapes=[pltpu.VMEM((tm, D_MODEL), BF16)],
        compiler_params=_params("parallel", "parallel"),
        name="mla_in",
    )(x, g, scale, shift, wq, wkv, wka, wkb, wz, q_norm, kv_norm, cc, ss)


def _mla_up_body(cq_ref, ckv_ref, kr_ref, cost_ref, sint_ref, wuqt_ref, wuk_ref, wuvt_ref,
                 qt_out, k_out, vt_out):
    cq = cq_ref[...]
    ckv = ckv_ref[...]
    cost = cost_ref[...]
    sint = sint_ref[...]
    qt = lax.dot_general(wuqt_ref[...], cq, NT_DIMS, preferred_element_type=F32)
    for h in range(N_HEADS):
        r0 = h * QK_DIM
        r1 = r0 + QK_NOPE_DIM
        r2 = r1 + HALF_ROPE
        r3 = r2 + HALF_ROPE
        x1 = qt[r1:r2]
        x2 = qt[r2:r3]
        qt_out[r0:r1, :] = (qt[r0:r1] * Q_PRESCALE).astype(BF16)
        qt_out[r1:r2, :] = ((x1 * cost - x2 * sint) * Q_PRESCALE).astype(BF16)
        qt_out[r2:r3, :] = ((x2 * cost + x1 * sint) * Q_PRESCALE).astype(BF16)
    kn = jnp.dot(ckv, wuk_ref[...], preferred_element_type=F32)
    kr = kr_ref[...]
    for h in range(N_HEADS):
        k_out[h, :, 0:QK_NOPE_DIM] = kn[:, h * QK_NOPE_DIM:(h + 1) * QK_NOPE_DIM].astype(BF16)
        k_out[h, :, QK_NOPE_DIM:QK_DIM] = kr
    vt = lax.dot_general(wuvt_ref[...], ckv, NT_DIMS, preferred_element_type=F32)
    vt_out[...] = vt.astype(BF16)


def _mla_up(cq, ckv, kr, cost, sint, wuqt, wuk, wuvt):
    b, l, _ = cq.shape
    tm = min(256, l)
    const = lambda bi, i: (0, 0)
    rows = lambda bi, i: (bi, i, 0)
    cols = lambda bi, i: (bi, 0, i)
    return pl.pallas_call(
        _mla_up_body,
        grid=(b, l // tm),
        in_specs=[pl.BlockSpec((None, tm, Q_LORA_RANK), rows),
                  pl.BlockSpec((None, tm, KV_LORA_RANK), rows),
                  pl.BlockSpec((None, tm, QK_ROPE_DIM), rows),
                  pl.BlockSpec((HALF_ROPE, tm), lambda bi, i: (0, i)),
                  pl.BlockSpec((HALF_ROPE, tm), lambda bi, i: (0, i)),
                  pl.BlockSpec((N_HEADS * QK_DIM, Q_LORA_RANK), const),
                  pl.BlockSpec((KV_LORA_RANK, N_HEADS * QK_NOPE_DIM), const),
                  pl.BlockSpec((MLA_WIDTH, KV_LORA_RANK), const)],
        out_specs=[pl.BlockSpec((None, N_HEADS * QK_DIM, tm), cols),
                   pl.BlockSpec((None, N_HEADS, tm, QK_DIM), lambda bi, i: (bi, 0, i, 0)),
                   pl.BlockSpec((None, MLA_WIDTH, tm), cols)],
        out_shape=[jax.ShapeDtypeStruct((b, N_HEADS * QK_DIM, l), BF16),
                   jax.ShapeDtypeStruct((b, N_HEADS, l, QK_DIM), BF16),
                   jax.ShapeDtypeStruct((b, MLA_WIDTH, l), BF16)],
        compiler_params=_params("parallel", "parallel"),
        name="mla_up",
    )(cq, ckv, kr, cost, sint, wuqt, wuk, wuvt)


def _attn_body(qt_ref, k_ref, vt_ref, zs_ref, o_ref, acc_ref, m_ref, l_ref, *stage_refs, nchunks, tk):
    bufs = tuple(stage_refs[4 * u:4 * u + 4] for u in range(ATTN_BUFS))

    def scores(c, s_ref, cm_ref):
        r = c * tk if isinstance(c, int) else pl.multiple_of(c * tk, tk)
        s = jnp.dot(k_ref[pl.ds(r, tk), :], qt_ref[...], preferred_element_type=F32)
        s_ref[...] = s
        cm_ref[...] = jnp.max(s, axis=0, keepdims=True)

    def softmax(s_ref, cm_ref, p_ref, al_ref):
        m_prev = m_ref[...]
        m_new = jnp.maximum(m_prev, cm_ref[...])
        alpha = jnp.exp2(m_prev - m_new)
        p = jnp.exp2(s_ref[...] - m_new)
        l_ref[...] = alpha * l_ref[...] + jnp.sum(p, axis=0, keepdims=True)
        p_ref[...] = p.astype(BF16)
        al_ref[...] = alpha
        m_ref[...] = m_new

    def values(c, p_ref, al_ref):
        r = c * tk if isinstance(c, int) else pl.multiple_of(c * tk, tk)
        pv = jnp.dot(vt_ref[:, pl.ds(r, tk)], p_ref[...], preferred_element_type=F32)
        acc_ref[...] = al_ref[...] * acc_ref[...] + pv

    m_ref[...] = jnp.full(m_ref.shape, -jnp.inf, F32)
    l_ref[...] = jnp.zeros(l_ref.shape, F32)
    acc_ref[...] = jnp.zeros(acc_ref.shape, F32)
    for c in range(min(ATTN_LOOKAHEAD, nchunks)):
        scores(c, *bufs[c % ATTN_BUFS][:2])
    for c in range(nchunks):
        s_cur, cm_cur, p_cur, al_cur = bufs[c % ATTN_BUFS]
        softmax(s_cur, cm_cur, p_cur, al_cur)
        if c > 0:
            values(c - 1, *bufs[(c - 1) % ATTN_BUFS][2:])
        if c + ATTN_LOOKAHEAD < nchunks:
            scores(c + ATTN_LOOKAHEAD, *bufs[(c + ATTN_LOOKAHEAD) % ATTN_BUFS][:2])
    values(nchunks - 1, *bufs[(nchunks - 1) % ATTN_BUFS][2:])
    o = acc_ref[...] / l_ref[...]
    o_ref[...] = (o.T * zs_ref[...]).astype(BF16)


def _attention(qt, k, vt, zs):
    b, _, l = qt.shape
    tq = min(512, l)
    tk = min(ATTN_KEY_CHUNK, l)
    nchunks = l // tk
    stat = pltpu.VMEM((1, tq), F32)
    return pl.pallas_call(
        functools.partial(_attn_body, nchunks=nchunks, tk=tk),
        grid=(b, N_HEADS, l // tq),
        in_specs=[pl.BlockSpec((None, QK_DIM, tq), lambda bi, h, qi: (bi, h, qi)),
                  pl.BlockSpec((None, None, l, QK_DIM), lambda bi, h, qi: (bi, h, 0, 0)),
                  pl.BlockSpec((None, V_HEAD_DIM, l), lambda bi, h, qi: (bi, h, 0)),
                  pl.BlockSpec((None, tq, V_HEAD_DIM), lambda bi, h, qi: (bi, qi, h))],
        out_specs=pl.BlockSpec((None, tq, V_HEAD_DIM), lambda bi, h, qi: (bi, qi, h)),
        out_shape=jax.ShapeDtypeStruct((b, l, MLA_WIDTH), BF16),
        scratch_shapes=[pltpu.VMEM((V_HEAD_DIM, tq), F32), stat, stat]
        + [pltpu.VMEM((tk, tq), F32), stat, pltpu.VMEM((tk, tq), BF16), stat] * ATTN_BUFS,
        compiler_params=_params("parallel", "parallel", "parallel"),
        name="mla_attn",
    )(qt, k, vt, zs)


def _rope_tables(length):
    inv = 1.0 / (ROPE_THETA ** (jnp.arange(0, QK_ROPE_DIM, 2, dtype=F32) / QK_ROPE_DIM))
    ang = jnp.arange(length, dtype=F32)[:, None] * inv[None, :]
    return jnp.cos(ang), jnp.sin(ang)


def _prep_weights(conv_w_in, conv_w_out, mla_w_in, mla_w_uq, mla_w_ukv, mla_w_out):
    q0, q1, q2 = Q_LORA_RANK, Q_LORA_RANK + KV_LORA_RANK, Q_LORA_RANK + KV_LORA_RANK + QK_ROPE_DIM
    pad = ((0, 0), (0, 0), (0, LANES - QK_ROPE_DIM))
    wkr = mla_w_in[:, :, q1:q2]
    wkr_swapped = jnp.concatenate([wkr[..., HALF_ROPE:], wkr[..., :HALF_ROPE]], axis=-1)
    n_mla = mla_w_ukv.shape[0]
    wukv = mla_w_ukv.reshape(n_mla, KV_LORA_RANK, N_HEADS, QK_NOPE_DIM + V_HEAD_DIM)
    return dict(
        conv_w_in=conv_w_in.astype(BF16),
        conv_w_out=conv_w_out.astype(BF16),
        wq=mla_w_in[:, :, :q0].astype(BF16),
        wkv=mla_w_in[:, :, q0:q1].astype(BF16),
        wka=jnp.pad(wkr, pad).astype(BF16),
        wkb=jnp.pad(wkr_swapped, pad).astype(BF16),
        wz=mla_w_in[:, :, q2:].astype(BF16),
        wuqt=jnp.swapaxes(mla_w_uq, 1, 2).astype(BF16),
        wuk=wukv[..., :QK_NOPE_DIM].reshape(n_mla, KV_LORA_RANK, N_HEADS * QK_NOPE_DIM).astype(BF16),
        wuvt=jnp.swapaxes(wukv[..., QK_NOPE_DIM:].reshape(n_mla, KV_LORA_RANK, MLA_WIDTH), 1, 2).astype(BF16),
        mla_w_out=mla_w_out.astype(BF16),
    )


def _trunk(x, mod, norm_g, wts, conv_dw_w, conv_dw_b, conv_ln_g, conv_ln_b,
           mla_q_norm, mla_kv_norm, final_g):
    assert DEPTH % 2 == 0
    length = x.shape[1]
    cos, sin = _rope_tables(length)
    lane_pad = ((0, 0), (0, LANES - QK_ROPE_DIM))
    cc = jnp.pad(jnp.concatenate([cos, cos], axis=-1), lane_pad)
    ss = jnp.pad(jnp.concatenate([-sin, sin], axis=-1), lane_pad)
    cost, sint = cos.T, sin.T
    for i in range(DEPTH):
        shift = mod[i, :, 0][:, None, :]
        scale = mod[i, :, 1][:, None, :]
        gate = mod[i, :, 2][:, None, :]
        g = norm_g[i][None, :]
        j = i // 2
        if i % 2 == 0:
            y, zs = _conv_in(x, g, scale, shift, wts["conv_w_in"][j])
            yc = _dwconv(y, zs, conv_dw_w[j], conv_dw_b[j], conv_ln_g[j], conv_ln_b[j])
            x = _out_proj(yc, wts["conv_w_out"][j], x, gate)
        else:
            cq, ckv, kr, zs = _mla_in(x, g, scale, shift, wts["wq"][j], wts["wkv"][j], wts["wka"][j],
                                      wts["wkb"][j], wts["wz"][j], mla_q_norm[j][None, :],
                                      mla_kv_norm[j][None, :], cc, ss)
            qt, k, vt = _mla_up(cq, ckv, kr, cost, sint, wts["wuqt"][j], wts["wuk"][j], wts["wuvt"][j])
            og = _attention(qt, k, vt, zs)
            x = _out_proj(og, wts["mla_w_out"][j], x, gate, final_g[None, :] if i == DEPTH - 1 else None)
    return x


def kernel(x_prompt, x_sample, c_prompt, c_sample, norm_g, ada_w, ada_b, conv_w_in, conv_dw_w, conv_dw_b,
           conv_ln_g, conv_ln_b, conv_w_out, mla_w_in, mla_q_norm, mla_kv_norm, mla_w_uq, mla_w_ukv,
           mla_w_out, final_g):
    nb_p, nb_s = c_prompt.shape[0], c_sample.shape[0]
    c_all = jnp.concatenate([c_prompt, c_sample, jnp.zeros((MOD_ROWS - nb_p - nb_s, D_MODEL), F32)], axis=0)
    mod = _ada_mod(c_all, ada_w, ada_b).reshape(DEPTH, MOD_ROWS, 3, D_MODEL)
    wts = _prep_weights(conv_w_in, conv_w_out, mla_w_in, mla_w_uq, mla_w_ukv, mla_w_out)
    rest = (norm_g, wts, conv_dw_w, conv_dw_b, conv_ln_g, conv_ln_b, mla_q_norm, mla_kv_norm, final_g)
    y_prompt = _trunk(x_prompt, mod[:, :nb_p], *rest)
    y_sample = _trunk(x_sample, mod[:, nb_p:nb_p + nb_s], *rest)
    return (y_prompt, y_sample)
```

```python
import functools
import math

import jax
import jax.numpy as jnp
from jax import lax
from jax.experimental import pallas as pl
from jax.experimental.pallas import tpu as pltpu

F32 = jnp.float32
BF16 = jnp.bfloat16

D_MODEL = 2048
DEPTH = 4
CONV_WIDTH = 4096
CONV_KERNEL = 31
CONV_HALO = 16
N_HEADS = 16
QK_NOPE_DIM = 128
QK_ROPE_DIM = 64
HALF_ROPE = QK_ROPE_DIM // 2
QK_DIM = QK_NOPE_DIM + QK_ROPE_DIM
V_HEAD_DIM = 128
Q_LORA_RANK = 512
KV_LORA_RANK = 512
MLA_WIDTH = N_HEADS * V_HEAD_DIM
ROPE_THETA = 10000.0
EPS = 1e-6
LANES = 128
SUBLANES = 8
LN_LANE_CHUNK = 512
MOD_ROWS = SUBLANES
VMEM_LIMIT_BYTES = 56 * 1024 * 1024
Q_PRESCALE = math.log2(math.e) / math.sqrt(QK_DIM)
NT_DIMS = (((1,), (1,)), ((), ()))
ATTN_LOOKAHEAD = 2
ATTN_BUFS = ATTN_LOOKAHEAD + 2
ATTN_KEY_CHUNK = 1024


def _params(*sem):
    return pltpu.CompilerParams(dimension_semantics=sem, vmem_limit_bytes=VMEM_LIMIT_BYTES)


def _silu(v):
    return v * jax.nn.sigmoid(v)


def _ada_body(c_ref, w_ref, b_ref, o_ref):
    c = c_ref[...]
    ca = _silu(c).astype(BF16)
    o_ref[0] = jnp.dot(ca, w_ref[0].astype(BF16), preferred_element_type=F32) + b_ref[0]


def _ada_mod(c_all, ada_w, ada_b):
    n = 3 * D_MODEL
    tn = 1024
    return pl.pallas_call(
        _ada_body,
        grid=(DEPTH, n // tn),
        in_specs=[pl.BlockSpec((MOD_ROWS, D_MODEL), lambda i, j: (0, 0)),
                  pl.BlockSpec((1, D_MODEL, tn), lambda i, j: (i, 0, j)),
                  pl.BlockSpec((1, 1, tn), lambda i, j: (i, 0, j))],
        out_specs=pl.BlockSpec((1, MOD_ROWS, tn), lambda i, j: (i, 0, j)),
        out_shape=jax.ShapeDtypeStruct((DEPTH, MOD_ROWS, n), F32),
        compiler_params=_params("parallel", "parallel"),
        name="ada_mod",
    )(c_all, ada_w, ada_b.reshape(DEPTH, 1, n))


def _modnorm_to(x_ref, g_ref, sc_ref, sh_ref, h_ref, tm, chunk):
    g = g_ref[...]
    sc = 1.0 + sc_ref[...]
    sh = sh_ref[...]

    def body(c, carry):
        r = pl.multiple_of(c * chunk, chunk)
        x = x_ref[pl.ds(r, chunk), :]
        ms = jnp.mean(x * x, axis=-1, keepdims=True)
        y = x * lax.rsqrt(ms + EPS) * g
        h_ref[pl.ds(r, chunk), :] = (y * sc + sh).astype(BF16)
        return carry

    lax.fori_loop(0, tm // chunk, body, 0)


def _conv_in_body(x_ref, g_ref, sc_ref, sh_ref, wa_ref, wg_ref, wz_ref, y_ref, z_ref, h_ref, *, tm):
    @pl.when(pl.program_id(2) == 0)
    def _():
        _modnorm_to(x_ref, g_ref, sc_ref, sh_ref, h_ref, tm, min(tm, 64))

    h = h_ref[...]
    a = jnp.dot(h, wa_ref[...], preferred_element_type=F32)
    gl = jnp.dot(h, wg_ref[...], preferred_element_type=F32)
    y_ref[...] = a * jax.nn.sigmoid(gl)
    z = jnp.dot(h, wz_ref[...], preferred_element_type=F32)
    z_ref[...] = _silu(z)


def _conv_in(x, g, scale, shift, w_in):
    b, l, _ = x.shape
    c = CONV_WIDTH
    tm = min(1024, l)
    tn = 512
    nj = c // tn
    return pl.pallas_call(
        functools.partial(_conv_in_body, tm=tm),
        grid=(b, l // tm, nj),
        in_specs=[pl.BlockSpec((None, tm, D_MODEL), lambda bi, i, j: (bi, i, 0)),
                  pl.BlockSpec((1, D_MODEL), lambda bi, i, j: (0, 0)),
                  pl.BlockSpec((None, 1, D_MODEL), lambda bi, i, j: (bi, 0, 0)),
                  pl.BlockSpec((None, 1, D_MODEL), lambda bi, i, j: (bi, 0, 0)),
                  pl.BlockSpec((D_MODEL, tn), lambda bi, i, j: (0, j)),
                  pl.BlockSpec((D_MODEL, tn), lambda bi, i, j: (0, j + nj)),
                  pl.BlockSpec((D_MODEL, tn), lambda bi, i, j: (0, j + 2 * nj))],
        out_specs=[pl.BlockSpec((None, tm, tn), lambda bi, i, j: (bi, i, j)),
                   pl.BlockSpec((None, tm, tn), lambda bi, i, j: (bi, i, j))],
        out_shape=[jax.ShapeDtypeStruct((b, l, c), F32), jax.ShapeDtypeStruct((b, l, c), F32)],
        scratch_shapes=[pltpu.VMEM((tm, D_MODEL), BF16)],
        compiler_params=_params("parallel", "parallel", "arbitrary"),
        name="conv_in",
    )(x, g, scale, shift, w_in, w_in, w_in)


def _dwconv_body(yp_ref, ym_ref, yn_ref, zs_ref, w_ref, b_ref, lg_ref, lb_ref, o_ref,
                 xbuf, shbuf, cbuf, *, tl, strip, ln_rows):
    i = pl.program_id(1)
    n = pl.num_programs(1)
    halo = CONV_HALO
    xbuf[0:halo, :] = jnp.where(i > 0, yp_ref[...], 0.0)
    xbuf[halo:halo + tl, :] = ym_ref[...]
    xbuf[halo + tl:2 * halo + tl, :] = jnp.where(i < n - 1, yn_ref[...], 0.0)

    first = halo - CONV_KERNEL // 2
    sh_rows = tl + 2 * halo - SUBLANES

    def cb_body(cb, carry):
        lanes = pl.ds(pl.multiple_of(cb * LANES, LANES), LANES)
        for r in range(1, SUBLANES):
            shbuf[r, :, :] = xbuf[pl.ds(r, sh_rows), lanes]
        bias = b_ref[:, lanes]
        for s in range(tl // strip):
            acc = jnp.broadcast_to(bias, (strip, LANES))
            for k in range(CONV_KERNEL):
                a8, r = divmod(first + k, SUBLANES)
                start = s * strip + SUBLANES * a8
                if r == 0:
                    win = xbuf[pl.ds(start, strip), lanes]
                else:
                    win = shbuf[r, pl.ds(start, strip), :]
                acc = acc + win * w_ref[k:k + 1, lanes]
            cbuf[pl.ds(s * strip, strip), lanes] = acc
        return carry

    lax.fori_loop(0, CONV_WIDTH // LANES, cb_body, 0)

    def ln_body(t, carry):
        rows = pl.ds(pl.multiple_of(t * ln_rows, ln_rows), ln_rows)
        v = cbuf[rows, :]
        mu = jnp.mean(v, axis=-1, keepdims=True)
        vc = v - mu
        rstd = lax.rsqrt(jnp.mean(vc * vc, axis=-1, keepdims=True) + EPS)

        def chunk_body(q, inner):
            lanes = pl.ds(pl.multiple_of(q * LN_LANE_CHUNK, LN_LANE_CHUNK), LN_LANE_CHUNK)
            yn = (cbuf[rows, lanes] - mu) * rstd * lg_ref[:, lanes] + lb_ref[:, lanes]
            o_ref[rows, lanes] = (_silu(yn) * zs_ref[rows, lanes]).astype(BF16)
            return inner

        lax.fori_loop(0, CONV_WIDTH // LN_LANE_CHUNK, chunk_body, 0)
        return carry

    lax.fori_loop(0, tl // ln_rows, ln_body, 0)


def _dwconv(y, zs, dw_w, dw_b, ln_g, ln_b):
    b, l, c = y.shape
    tl = min(256, l)
    hb = tl // CONV_HALO
    nhb = l // CONV_HALO
    body = functools.partial(_dwconv_body, tl=tl, strip=min(64, tl), ln_rows=min(64, tl))
    row = lambda bi, i: (0, 0)
    return pl.pallas_call(
        body,
        grid=(b, l // tl),
        in_specs=[pl.BlockSpec((None, CONV_HALO, c), lambda bi, i: (bi, jnp.maximum(i * hb - 1, 0), 0)),
                  pl.BlockSpec((None, tl, c), lambda bi, i: (bi, i, 0)),
                  pl.BlockSpec((None, CONV_HALO, c), lambda bi, i: (bi, jnp.minimum((i + 1) * hb, nhb - 1), 0)),
                  pl.BlockSpec((None, tl, c), lambda bi, i: (bi, i, 0)),
                  pl.BlockSpec((CONV_KERNEL, c), row),
                  pl.BlockSpec((1, c), row),
                  pl.BlockSpec((1, c), row),
                  pl.BlockSpec((1, c), row)],
        out_specs=pl.BlockSpec((None, tl, c), lambda bi, i: (bi, i, 0)),
        out_shape=jax.ShapeDtypeStruct((b, l, c), BF16),
        scratch_shapes=[pltpu.VMEM((tl + 2 * CONV_HALO, c), F32),
                        pltpu.VMEM((SUBLANES, tl + 2 * CONV_HALO - SUBLANES, LANES), F32),
                        pltpu.VMEM((tl, c), F32)],
        compiler_params=_params("parallel", "parallel"),
        name="dwconv_ln",
    )(y, y, y, zs, dw_w, dw_b.reshape(1, c), ln_g.reshape(1, c), ln_b.reshape(1, c))


def _out_proj_body(a_ref, w_ref, x_ref, gt_ref, o_ref):
    y = jnp.dot(a_ref[...], w_ref[...], preferred_element_type=F32)
    o_ref[...] = x_ref[...] + gt_ref[...] * y


def _out_proj_norm_body(a_ref, w_ref, x_ref, gt_ref, g_ref, o_ref):
    y = jnp.dot(a_ref[...], w_ref[...], preferred_element_type=F32)
    xn = x_ref[...] + gt_ref[...] * y
    ms = jnp.mean(xn * xn, axis=-1, keepdims=True)
    o_ref[...] = xn * lax.rsqrt(ms + EPS) * g_ref[...]


def _out_proj(a, w, x, gate, final_g=None):
    b, l, k = a.shape
    if final_g is None:
        tm, tn, body, extra, extra_specs = min(1024, l), 512, _out_proj_body, (), []
    else:
        tm, tn, body, extra = min(512, l), D_MODEL, _out_proj_norm_body, (final_g,)
        extra_specs = [pl.BlockSpec((1, D_MODEL), lambda bi, i, j: (0, 0))]
    return pl.pallas_call(
        body,
        grid=(b, l // tm, D_MODEL // tn),
        in_specs=[pl.BlockSpec((None, tm, k), lambda bi, i, j: (bi, i, 0)),
                  pl.BlockSpec((k, tn), lambda bi, i, j: (0, j)),
                  pl.BlockSpec((None, tm, tn), lambda bi, i, j: (bi, i, j)),
                  pl.BlockSpec((None, 1, tn), lambda bi, i, j: (bi, 0, j))] + extra_specs,
        out_specs=pl.BlockSpec((None, tm, tn), lambda bi, i, j: (bi, i, j)),
        out_shape=jax.ShapeDtypeStruct((b, l, D_MODEL), F32),
        compiler_params=_params("parallel", "parallel", "parallel"),
        name="out_proj",
    )(a, w, x, gate, *extra)


def _mla_in_body(x_ref, g_ref, sc_ref, sh_ref, wq_ref, wkv_ref, wka_ref, wkb_ref, wz_ref,
                 qn_ref, kvn_ref, cc_ref, ss_ref,
                 cq_out, ckv_out, kr_out, zs_out, h_ref, *, tm, zchunk):
    _modnorm_to(x_ref, g_ref, sc_ref, sh_ref, h_ref, tm, min(tm, 64))
    h = h_ref[...]

    def rms(v, gain):
        ms = jnp.mean(v * v, axis=-1, keepdims=True)
        return (v * lax.rsqrt(ms + EPS) * gain).astype(BF16)

    cq_out[...] = rms(jnp.dot(h, wq_ref[...], preferred_element_type=F32), qn_ref[...])
    ckv_out[...] = rms(jnp.dot(h, wkv_ref[...], preferred_element_type=F32), kvn_ref[...])
    ka = jnp.dot(h, wka_ref[...], preferred_element_type=F32)
    kb = jnp.dot(h, wkb_ref[...], preferred_element_type=F32)
    kr = ka * cc_ref[...] + kb * ss_ref[...]
    kr_out[...] = kr[:, :QK_ROPE_DIM].astype(BF16)
    for c in range(MLA_WIDTH // zchunk):
        z = jnp.dot(h, wz_ref[:, c * zchunk:(c + 1) * zchunk], preferred_element_type=F32)
        zs_out[:, c * zchunk:(c + 1) * zchunk] = _silu(z)


def _mla_in(x, g, scale, shift, wq, wkv, wka, wkb, wz, q_norm, kv_norm, cc, ss):
    b, l, _ = x.shape
    tm = min(256, l)
    const = lambda bi, i: (0, 0)
    rows = lambda bi, i: (bi, i, 0)
    per_b = lambda bi, i: (bi, 0, 0)
    return pl.pallas_call(
        functools.partial(_mla_in_body, tm=tm, zchunk=512),
        grid=(b, l // tm),
        in_specs=[pl.BlockSpec((None, tm, D_MODEL), rows),
                  pl.BlockSpec((1, D_MODEL), const),
                  pl.BlockSpec((None, 1, D_MODEL), per_b),
                  pl.BlockSpec((None, 1, D_MODEL), per_b),
                  pl.BlockSpec((D_MODEL, Q_LORA_RANK), const),
                  pl.BlockSpec((D_MODEL, KV_LORA_RANK), const),
                  pl.BlockSpec((D_MODEL, LANES), const),
                  pl.BlockSpec((D_MODEL, LANES), const),
                  pl.BlockSpec((D_MODEL, MLA_WIDTH), const),
                  pl.BlockSpec((1, Q_LORA_RANK), const),
                  pl.BlockSpec((1, KV_LORA_RANK), const),
                  pl.BlockSpec((tm, LANES), lambda bi, i: (i, 0)),
                  pl.BlockSpec((tm, LANES), lambda bi, i: (i, 0))],
        out_specs=[pl.BlockSpec((None, tm, Q_LORA_RANK), rows),
                   pl.BlockSpec((None, tm, KV_LORA_RANK), rows),
                   pl.BlockSpec((None, tm, QK_ROPE_DIM), rows),
                   pl.BlockSpec((None, tm, MLA_WIDTH), rows)],
        out_shape=[jax.ShapeDtypeStruct((b, l, Q_LORA_RANK), BF16),
                   jax.ShapeDtypeStruct((b, l, KV_LORA_RANK), BF16),
                   jax.ShapeDtypeStruct((b, l, QK_ROPE_DIM), BF16),
                   jax.ShapeDtypeStruct((b, l, MLA_WIDTH), F32)],
        scratch_shapes=[pltpu.VMEM((tm, D_MODEL), BF16)],
        compiler_params=_params("parallel", "parallel"),
        name="mla_in",
    )(x, g, scale, shift, wq, wkv, wka, wkb, wz, q_norm, kv_norm, cc, ss)


def _mla_up_body(cq_ref, ckv_ref, kr_ref, cost_ref, sint_ref, wuqt_ref, wuk_ref, wuvt_ref,
                 qt_out, k_out, vt_out):
    cq = cq_ref[...]
    ckv = ckv_ref[...]
    cost = cost_ref[...]
    sint = sint_ref[...]
    qt = lax.dot_general(wuqt_ref[...], cq, NT_DIMS, preferred_element_type=F32)
    for h in range(N_HEADS):
        r0 = h * QK_DIM
        r1 = r0 + QK_NOPE_DIM
        r2 = r1 + HALF_ROPE
        r3 = r2 + HALF_ROPE
        x1 = qt[r1:r2]
        x2 = qt[r2:r3]
        qt_out[r0:r1, :] = (qt[r0:r1] * Q_PRESCALE).astype(BF16)
        qt_out[r1:r2, :] = ((x1 * cost - x2 * sint) * Q_PRESCALE).astype(BF16)
        qt_out[r2:r3, :] = ((x2 * cost + x1 * sint) * Q_PRESCALE).astype(BF16)
    kn = jnp.dot(ckv, wuk_ref[...], preferred_element_type=F32)
    kr = kr_ref[...]
    for h in range(N_HEADS):
        k_out[h, :, 0:QK_NOPE_DIM] = kn[:, h * QK_NOPE_DIM:(h + 1) * QK_NOPE_DIM].astype(BF16)
        k_out[h, :, QK_NOPE_DIM:QK_DIM] = kr
    vt = lax.dot_general(wuvt_ref[...], ckv, NT_DIMS, preferred_element_type=F32)
    vt_out[...] = vt.astype(BF16)


def _mla_up(cq, ckv, kr, cost, sint, wuqt, wuk, wuvt):
    b, l, _ = cq.shape
    tm = min(256, l)
    const = lambda bi, i: (0, 0)
    rows = lambda bi, i: (bi, i, 0)
    cols = lambda bi, i: (bi, 0, i)
    return pl.pallas_call(
        _mla_up_body,
        grid=(b, l // tm),
        in_specs=[pl.BlockSpec((None, tm, Q_LORA_RANK), rows),
                  pl.BlockSpec((None, tm, KV_LORA_RANK), rows),
                  pl.BlockSpec((None, tm, QK_ROPE_DIM), rows),
                  pl.BlockSpec((HALF_ROPE, tm), lambda bi, i: (0, i)),
                  pl.BlockSpec((HALF_ROPE, tm), lambda bi, i: (0, i)),
                  pl.BlockSpec((N_HEADS * QK_DIM, Q_LORA_RANK), const),
                  pl.BlockSpec((KV_LORA_RANK, N_HEADS * QK_NOPE_DIM), const),
                  pl.BlockSpec((MLA_WIDTH, KV_LORA_RANK), const)],
        out_specs=[pl.BlockSpec((None, N_HEADS * QK_DIM, tm), cols),
                   pl.BlockSpec((None, N_HEADS, tm, QK_DIM), lambda bi, i: (bi, 0, i, 0)),
                   pl.BlockSpec((None, MLA_WIDTH, tm), cols)],
        out_shape=[jax.ShapeDtypeStruct((b, N_HEADS * QK_DIM, l), BF16),
                   jax.ShapeDtypeStruct((b, N_HEADS, l, QK_DIM), BF16),
                   jax.ShapeDtypeStruct((b, MLA_WIDTH, l), BF16)],
        compiler_params=_params("parallel", "parallel"),
        name="mla_up",
    )(cq, ckv, kr, cost, sint, wuqt, wuk, wuvt)


def _attn_body(qt_ref, k_ref, vt_ref, zs_ref, o_ref, acc_ref, m_ref, l_ref, *stage_refs, nchunks, tk):
    bufs = tuple(stage_refs[4 * u:4 * u + 4] for u in range(ATTN_BUFS))

    def scores(c, s_ref, cm_ref):
        r = c * tk if isinstance(c, int) else pl.multiple_of(c * tk, tk)
        s = jnp.dot(k_ref[pl.ds(r, tk), :], qt_ref[...], preferred_element_type=F32)
        s_ref[...] = s
        cm_ref[...] = jnp.max(s, axis=0, keepdims=True)

    def softmax(s_ref, cm_ref, p_ref, al_ref):
        m_prev = m_ref[...]
        m_new = jnp.maximum(m_prev, cm_ref[...])
        alpha = jnp.exp2(m_prev - m_new)
        p = jnp.exp2(s_ref[...] - m_new)
        l_ref[...] = alpha * l_ref[...] + jnp.sum(p, axis=0, keepdims=True)
        p_ref[...] = p.astype(BF16)
        al_ref[...] = alpha
        m_ref[...] = m_new

    def values(c, p_ref, al_ref):
        r = c * tk if isinstance(c, int) else pl.multiple_of(c * tk, tk)
        pv = jnp.dot(vt_ref[:, pl.ds(r, tk)], p_ref[...], preferred_element_type=F32)
        acc_ref[...] = al_ref[...] * acc_ref[...] + pv

    m_ref[...] = jnp.full(m_ref.shape, -jnp.inf, F32)
    l_ref[...] = jnp.zeros(l_ref.shape, F32)
    acc_ref[...] = jnp.zeros(acc_ref.shape, F32)
    for c in range(min(ATTN_LOOKAHEAD, nchunks)):
        scores(c, *bufs[c % ATTN_BUFS][:2])
    for c in range(nchunks):
        s_cur, cm_cur, p_cur, al_cur = bufs[c % ATTN_BUFS]
        softmax(s_cur, cm_cur, p_cur, al_cur)
        if c > 0:
            values(c - 1, *bufs[(c - 1) % ATTN_BUFS][2:])
        if c + ATTN_LOOKAHEAD < nchunks:
            scores(c + ATTN_LOOKAHEAD, *bufs[(c + ATTN_LOOKAHEAD) % ATTN_BUFS][:2])
    values(nchunks - 1, *bufs[(nchunks - 1) % ATTN_BUFS][2:])
    o = acc_ref[...] / l_ref[...]
    o_ref[...] = (o.T * zs_ref[...]).astype(BF16)


def _attention(qt, k, vt, zs):
    b, _, l = qt.shape
    tq = min(512, l)
    tk = min(ATTN_KEY_CHUNK, l)
    nchunks = l // tk
    stat = pltpu.VMEM((1, tq), F32)
    return pl.pallas_call(
        functools.partial(_attn_body, nchunks=nchunks, tk=tk),
        grid=(b, N_HEADS, l // tq),
        in_specs=[pl.BlockSpec((None, QK_DIM, tq), lambda bi, h, qi: (bi, h, qi)),
                  pl.BlockSpec((None, None, l, QK_DIM), lambda bi, h, qi: (bi, h, 0, 0)),
                  pl.BlockSpec((None, V_HEAD_DIM, l), lambda bi, h, qi: (bi, h, 0)),
                  pl.BlockSpec((None, tq, V_HEAD_DIM), lambda bi, h, qi: (bi, qi, h))],
        out_specs=pl.BlockSpec((None, tq, V_HEAD_DIM), lambda bi, h, qi: (bi, qi, h)),
        out_shape=jax.ShapeDtypeStruct((b, l, MLA_WIDTH), BF16),
        scratch_shapes=[pltpu.VMEM((V_HEAD_DIM, tq), F32), stat, stat]
        + [pltpu.VMEM((tk, tq), F32), stat, pltpu.VMEM((tk, tq), BF16), stat] * ATTN_BUFS,
        compiler_params=_params("parallel", "parallel", "parallel"),
        name="mla_attn",
    )(qt, k, vt, zs)


def _rope_tables(length):
    inv = 1.0 / (ROPE_THETA ** (jnp.arange(0, QK_ROPE_DIM, 2, dtype=F32) / QK_ROPE_DIM))
    ang = jnp.arange(length, dtype=F32)[:, None] * inv[None, :]
    return jnp.cos(ang), jnp.sin(ang)


def _prep_weights(conv_w_in, conv_w_out, mla_w_in, mla_w_uq, mla_w_ukv, mla_w_out):
    q0, q1, q2 = Q_LORA_RANK, Q_LORA_RANK + KV_LORA_RANK, Q_LORA_RANK + KV_LORA_RANK + QK_ROPE_DIM
    pad = ((0, 0), (0, 0), (0, LANES - QK_ROPE_DIM))
    wkr = mla_w_in[:, :, q1:q2]
    wkr_swapped = jnp.concatenate([wkr[..., HALF_ROPE:], wkr[..., :HALF_ROPE]], axis=-1)
    n_mla = mla_w_ukv.shape[0]
    wukv = mla_w_ukv.reshape(n_mla, KV_LORA_RANK, N_HEADS, QK_NOPE_DIM + V_HEAD_DIM)
    return dict(
        conv_w_in=conv_w_in.astype(BF16),
        conv_w_out=conv_w_out.astype(BF16),
        wq=mla_w_in[:, :, :q0].astype(BF16),
        wkv=mla_w_in[:, :, q0:q1].astype(BF16),
        wka=jnp.pad(wkr, pad).astype(BF16),
        wkb=jnp.pad(wkr_swapped, pad).astype(BF16),
        wz=mla_w_in[:, :, q2:].astype(BF16),
        wuqt=jnp.swapaxes(mla_w_uq, 1, 2).astype(BF16),
        wuk=wukv[..., :QK_NOPE_DIM].reshape(n_mla, KV_LORA_RANK, N_HEADS * QK_NOPE_DIM).astype(BF16),
        wuvt=jnp.swapaxes(wukv[..., QK_NOPE_DIM:].reshape(n_mla, KV_LORA_RANK, MLA_WIDTH), 1, 2).astype(BF16),
        mla_w_out=mla_w_out.astype(BF16),
    )


def _trunk(x, mod, norm_g, wts, conv_dw_w, conv_dw_b, conv_ln_g, conv_ln_b,
           mla_q_norm, mla_kv_norm, final_g):
    assert DEPTH % 2 == 0
    length = x.shape[1]
    cos, sin = _rope_tables(length)
    lane_pad = ((0, 0), (0, LANES - QK_ROPE_DIM))
    cc = jnp.pad(jnp.concatenate([cos, cos], axis=-1), lane_pad)
    ss = jnp.pad(jnp.concatenate([-sin, sin], axis=-1), lane_pad)
    cost, sint = cos.T, sin.T
    for i in range(DEPTH):
        shift = mod[i, :, 0][:, None, :]
        scale = mod[i, :, 1][:, None, :]
        gate = mod[i, :, 2][:, None, :]
        g = norm_g[i][None, :]
        j = i // 2
        if i % 2 == 0:
            y, zs = _conv_in(x, g, scale, shift, wts["conv_w_in"][j])
            yc = _dwconv(y, zs, conv_dw_w[j], conv_dw_b[j], conv_ln_g[j], conv_ln_b[j])
            x = _out_proj(yc, wts["conv_w_out"][j], x, gate)
        else:
            cq, ckv, kr, zs = _mla_in(x, g, scale, shift, wts["wq"][j], wts["wkv"][j], wts["wka"][j],
                                      wts["wkb"][j], wts["wz"][j], mla_q_norm[j][None, :],
                                      mla_kv_norm[j][None, :], cc, ss)
            qt, k, vt = _mla_up(cq, ckv, kr, cost, sint, wts["wuqt"][j], wts["wuk"][j], wts["wuvt"][j])
            og = _attention(qt, k, vt, zs)
            x = _out_proj(og, wts["mla_w_out"][j], x, gate, final_g[None, :] if i == DEPTH - 1 else None)
    return x


def kernel(x_prompt, x_sample, c_prompt, c_sample, norm_g, ada_w, ada_b, conv_w_in, conv_dw_w, conv_dw_b,
           conv_ln_g, conv_ln_b, conv_w_out, mla_w_in, mla_q_norm, mla_kv_norm, mla_w_uq, mla_w_ukv,
           mla_w_out, final_g):
    nb_p, nb_s = c_prompt.shape[0], c_sample.shape[0]
    c_all = jnp.concatenate([c_prompt, c_sample, jnp.zeros((MOD_ROWS - nb_p - nb_s, D_MODEL), F32)], axis=0)
    mod = _ada_mod(c_all, ada_w, ada_b).reshape(DEPTH, MOD_ROWS, 3, D_MODEL)
    wts = _prep_weights(conv_w_in, conv_w_out, mla_w_in, mla_w_uq, mla_w_ukv, mla_w_out)
    rest = (norm_g, wts, conv_dw_w, conv_dw_b, conv_ln_g, conv_ln_b, mla_q_norm, mla_kv_norm, final_g)
    y_prompt = _trunk(x_prompt, mod[:, :nb_p], *rest)
    y_sample = _trunk(x_sample, mod[:, nb_p:nb_p + nb_s], *rest)
    return (y_prompt, y_sample)
```
